```python
import math
import jax
import jax.numpy as jnp
from jax import lax
import numpy as np

D_MODEL = 1024
BATCH = 8
SEQ = 4096
DEPTH = 2

PLE_DIM = 256
CONV_K = 4
NORM_EPS = 1e-6
N_BRANCH = 3
D_FF = 4 * D_MODEL

GDN_HEADS = 8
GDN_DK = 64
GDN_DV = 64
GDN_CHUNK = 64
GDN_QK = GDN_HEADS * GDN_DK
GDN_VW = GDN_HEADS * GDN_DV
GDN_QKV = 2 * GDN_QK + GDN_VW

SSM_HEADS = 8
SSM_HEAD_DIM = 64
SSM_GROUPS = 2
SSM_STATE = 64
SSM_CHUNK = 128
SSM_INNER = SSM_HEADS * SSM_HEAD_DIM
SSM_XBC = SSM_INNER + 2 * SSM_GROUPS * SSM_STATE

RET_HEADS = 8
RET_DK = 64
RET_DV = 64
RET_CHUNK = 128
RET_QK = RET_HEADS * RET_DK
RET_VW = RET_HEADS * RET_DV
ROPE_BASE = 10000.0

BRANCH_W = 512

IN_SPLITS = (
    GDN_QKV, GDN_VW, GDN_HEADS, GDN_HEADS,
    SSM_INNER, SSM_XBC, SSM_HEADS,
    RET_QK, RET_QK, RET_VW, RET_VW,
    N_BRANCH * D_MODEL,
)

kernel_name = "hybrid_gdn_ssd_retention_block"

F32 = jnp.float32


def rmsnorm(x, w=None, eps=NORM_EPS):
    xf = x.astype(F32)
    y = xf * lax.rsqrt(jnp.mean(xf * xf, axis=-1, keepdims=True) + eps)
    if w is not None:
        y = y * w.astype(F32)
    return y.astype(x.dtype)


def l2norm(x, eps=1e-6):
    return x * lax.rsqrt(jnp.sum(x * x, axis=-1, keepdims=True) + eps)


def split_cols(t, sizes):
    idx = np.cumsum(np.array(sizes))[:-1].tolist()
    return jnp.split(t, idx, axis=-1)


def causal_dwconv(x, w, b=None):
    K, C = w.shape
    y = lax.conv_general_dilated(
        x, w[:, None, :].astype(x.dtype), window_strides=(1,), padding=[(K - 1, 0)],
        dimension_numbers=('NWC', 'WIO', 'NWC'), feature_group_count=C)
    if b is not None:
        y = y + b.astype(y.dtype)
    return y


def to_chunks(t, L):
    Bz, S, H = t.shape[:3]
    t = t.reshape((Bz, S // L, L, H) + t.shape[3:])
    return jnp.moveaxis(t, 3, 1)


def from_chunks(t):
    Bz, H, NC, L = t.shape[:4]
    t = jnp.moveaxis(t, 1, 3)
    return t.reshape((Bz, NC * L, H) + t.shape[4:])


def rotary(t, pos):
    half = t.shape[-1] // 2
    inv_freq = ROPE_BASE ** (-jnp.linspace(0.0, 1.0, half, dtype=F32))
    ang = pos[:, None] * inv_freq[None, :]
    cos = jnp.cos(ang)[:, None, :]
    sin = jnp.sin(ang)[:, None, :]
    t1, t2 = t[..., :half], t[..., half:]
    return jnp.concatenate([t1 * cos - t2 * sin, t1 * sin + t2 * cos], axis=-1)


def chunk_gated_delta_rule(q, k, v, g, beta):
    Bz, S, H, DK = q.shape
    DV = v.shape[-1]
    L = GDN_CHUNK
    q, k, v, g, beta = (to_chunks(t, L) for t in (q, k, v, g, beta))
    g = jnp.cumsum(g, axis=-1)
    tril = jnp.tril(jnp.ones((L, L), bool))
    strict = jnp.tril(jnp.ones((L, L), bool), -1)
    decay = jnp.exp(jnp.where(tril, g[..., :, None] - g[..., None, :], -jnp.inf))
    kb = k * beta[..., None]
    vb = v * beta[..., None]
    A = jnp.where(strict, jnp.einsum('bhcld,bhcsd->bhcls', kb, k) * decay, 0.0)
    eye = jnp.eye(L, dtype=F32)
    rhs = jnp.concatenate([vb, kb * jnp.exp(g)[..., None]], axis=-1)
    sol = lax.linalg.triangular_solve(eye + A, rhs, left_side=True, lower=True, unit_diagonal=True)
    u, w = sol[..., :DV], sol[..., DV:]
    attn = jnp.einsum('bhcld,bhcsd->bhcls', q, k) * decay
    q_dec = q * jnp.exp(g)[..., None]
    k_dec = k * jnp.exp(g[..., -1:] - g)[..., None]
    g_last = jnp.exp(g[..., -1])
    xs = tuple(jnp.moveaxis(t, 2, 0) for t in (u, w, attn, q_dec, k_dec, g_last))

    def step(state, inp):
        u_c, w_c, attn_c, qd_c, kd_c, gl_c = inp
        v_new = u_c - jnp.einsum('bhld,bhde->bhle', w_c, state)
        o_c = jnp.einsum('bhld,bhde->bhle', qd_c, state) + jnp.einsum('bhls,bhse->bhle', attn_c, v_new)
        state = state * gl_c[..., None, None] + jnp.einsum('bhld,bhle->bhde', kd_c, v_new)
        return state, o_c

    state0 = jnp.zeros((Bz, H, DK, DV), F32)
    _, o = lax.scan(step, state0, xs)
    return from_chunks(jnp.moveaxis(o, 0, 2))


def gated_deltanet(qkv, z, b, a, conv_w, A_log, dt_bias, norm_w):
    Bz, S, _ = qkv.shape
    H = GDN_HEADS
    qkv = jax.nn.silu(causal_dwconv(qkv, conv_w)).astype(F32)
    q, k, v = jnp.split(qkv, [GDN_QK, 2 * GDN_QK], axis=-1)
    q = l2norm(q.reshape(Bz, S, H, GDN_DK)) * (GDN_DK ** -0.5)
    k = l2norm(k.reshape(Bz, S, H, GDN_DK))
    v = v.reshape(Bz, S, H, GDN_DV)
    beta = jax.nn.sigmoid(b.astype(F32))
    g = -jnp.exp(A_log.astype(F32)) * jax.nn.softplus(a.astype(F32) + dt_bias.astype(F32))
    o = chunk_gated_delta_rule(q, k, v, g, beta)
    o = rmsnorm(o, norm_w) * jax.nn.silu(z.astype(F32).reshape(Bz, S, H, GDN_DV))
    return o.reshape(Bz, S, GDN_VW)


def ssd_chunked(X, a, Bm, Cm):
    Bz, S, H, P = X.shape
    G, N = Bm.shape[2:]
    R = H // G
    L = SSM_CHUNK
    NC = S // L
    X = X.reshape(Bz, NC, L, G, R, P)
    a = a.reshape(Bz, NC, L, G, R)
    Bm = Bm.reshape(Bz, NC, L, G, N)
    Cm = Cm.reshape(Bz, NC, L, G, N)
    a_cs = jnp.cumsum(a, axis=2)
    tril = jnp.tril(jnp.ones((L, L), bool))[:, :, None, None]
    seg = a_cs[:, :, :, None] - a_cs[:, :, None, :]
    Lmat = jnp.exp(jnp.where(tril, seg, -jnp.inf))
    CB = jnp.einsum('bclgn,bcsgn->bclsg', Cm, Bm)
    y_diag = jnp.einsum('bclsg,bclsgr,bcsgrp->bclgrp', CB, Lmat, X)
    decay_states = jnp.exp(a_cs[:, :, -1:] - a_cs)
    states = jnp.einsum('bclgn,bclgr,bclgrp->bcgrpn', Bm, decay_states, X)
    chunk_decay = jnp.exp(a_cs[:, :, -1])

    def step(h, inp):
        s_c, d_c = inp
        return h * d_c[..., None, None] + s_c, h

    h0 = jnp.zeros((Bz, G, R, P, N), F32)
    _, h_in = lax.scan(step, h0, (jnp.moveaxis(states, 1, 0), jnp.moveaxis(chunk_decay, 1, 0)))
    h_in = jnp.moveaxis(h_in, 0, 1)
    y_off = jnp.einsum('bclgn,bcgrpn,bclgr->bclgrp', Cm, h_in, jnp.exp(a_cs))
    return (y_diag + y_off).reshape(Bz, S, H, P)


def mamba2_ssd(z, xbc, dt, conv_w, conv_b, A_log, dt_bias, D, norm_w):
    Bz, S, _ = z.shape
    G = SSM_GROUPS
    xbc = jax.nn.silu(causal_dwconv(xbc, conv_w, conv_b)).astype(F32)
    xs, Bm, Cm = jnp.split(xbc, [SSM_INNER, SSM_INNER + G * SSM_STATE], axis=-1)
    xs = xs.reshape(Bz, S, SSM_HEADS, SSM_HEAD_DIM)
    Bm = Bm.reshape(Bz, S, G, SSM_STATE)
    Cm = Cm.reshape(Bz, S, G, SSM_STATE)
    dt = jax.nn.softplus(dt.astype(F32) + dt_bias.astype(F32))
    A = -jnp.exp(A_log.astype(F32))
    y = ssd_chunked(xs * dt[..., None], A * dt, Bm, Cm)
    y = y + D.astype(F32)[:, None] * xs
    y = y.reshape(Bz, S, SSM_INNER) * jax.nn.silu(z.astype(F32))
    gs = SSM_INNER // G
    y = rmsnorm(y.reshape(Bz, S, G, gs), norm_w.reshape(G, gs))
    return y.reshape(Bz, S, SSM_INNER)


def retention(q, k, v, gate):
    Bz, S, _ = q.shape
    H, L = RET_HEADS, RET_CHUNK
    pos = jnp.arange(S, dtype=F32)
    q = rotary(q.astype(F32).reshape(Bz, S, H, RET_DK), pos)
    k = rotary(k.astype(F32).reshape(Bz, S, H, RET_DK), pos) * (RET_DK ** -0.5)
    v = v.astype(F32).reshape(Bz, S, H, RET_DV)
    q, k, v = (to_chunks(t, L) for t in (q, k, v))
    log_gamma = jnp.log1p(-jnp.exp2(-5.0 - jnp.arange(H, dtype=F32)))
    idx = jnp.arange(L, dtype=F32)
    tril = jnp.tril(jnp.ones((L, L), bool))
    dmat = jnp.exp(jnp.where(tril, (idx[:, None] - idx[None, :]) * log_gamma[:, None, None], -jnp.inf))
    attn = jnp.einsum('bhcld,bhcsd->bhcls', q, k) * dmat[:, None]
    y_in = jnp.einsum('bhcls,bhcse->bhcle', attn, v)
    k_dec = k * jnp.exp((L - 1 - idx)[None, :] * log_gamma[:, None])[:, None, :, None]
    kv = jnp.einsum('bhcld,bhcle->bhcde', k_dec, v)
    chunk_gamma = jnp.exp(L * log_gamma)

    def step(state, kv_c):
        return state * chunk_gamma[:, None, None] + kv_c, state

    s0 = jnp.zeros((Bz, H, RET_DK, RET_DV), F32)
    _, s_in = lax.scan(step, s0, jnp.moveaxis(kv, 2, 0))
    s_in = jnp.moveaxis(s_in, 0, 2)
    q_dec = q * jnp.exp((idx + 1)[None, :] * log_gamma[:, None])[:, None, :, None]
    y = from_chunks(y_in + jnp.einsum('bhcld,bhcde->bhcle', q_dec, s_in))
    y = rmsnorm(y) * jax.nn.silu(gate.astype(F32).reshape(Bz, S, H, RET_DV))
    return y.reshape(Bz, S, RET_VW)


def setup_inputs(seed: int = 0) -> dict:
    key = jax.random.key(seed)
    ks = jax.random.split(key, 24)

    def nrm(k, shape, fan_in):
        return jax.random.normal(k, shape, F32) * (fan_in ** -0.5)

    def gain(k, shape):
        return 1.0 + 0.02 * jax.random.normal(k, shape, F32)

    def dt_bias_init(k, shape):
        u = jax.random.uniform(k, shape, F32)
        dt = jnp.exp(u * (math.log(0.1) - math.log(0.001)) + math.log(0.001))
        return dt + jnp.log(-jnp.expm1(-dt))

    def a_log_init(k, shape):
        return jnp.log(jax.random.uniform(k, shape, F32, 1.0, 16.0))

    n_in = sum(IN_SPLITS)
    return {
        'x': jax.random.normal(ks[0], (BATCH, SEQ, D_MODEL), F32),
        'p': jax.random.normal(ks[1], (DEPTH, BATCH, SEQ, PLE_DIM), F32),
        'norm_mix': gain(ks[2], (DEPTH, D_MODEL)),
        'w_in': nrm(ks[3], (DEPTH, D_MODEL, n_in), D_MODEL),
        'gdn_conv_w': nrm(ks[4], (DEPTH, CONV_K, GDN_QKV), CONV_K),
        'gdn_A_log': a_log_init(ks[5], (DEPTH, GDN_HEADS)),
        'gdn_dt_bias': dt_bias_init(ks[6], (DEPTH, GDN_HEADS)),
        'gdn_norm_w': gain(ks[7], (DEPTH, GDN_DV)),
        'ssm_conv_w': nrm(ks[8], (DEPTH, CONV_K, SSM_XBC), CONV_K),
        'ssm_conv_b': 0.02 * jax.random.normal(ks[9], (DEPTH, SSM_XBC), F32),
        'ssm_A_log': a_log_init(ks[10], (DEPTH, SSM_HEADS)),
        'ssm_dt_bias': dt_bias_init(ks[11], (DEPTH, SSM_HEADS)),
        'ssm_D': gain(ks[12], (DEPTH, SSM_HEADS)),
        'ssm_norm_w': gain(ks[13], (DEPTH, SSM_INNER)),
        'w_branch': nrm(ks[14], (DEPTH, N_BRANCH, BRANCH_W, D_MODEL), BRANCH_W),
        'w_out': nrm(ks[15], (DEPTH, D_MODEL, D_MODEL), D_MODEL),
        'norm_mlp': gain(ks[16], (DEPTH, D_MODEL)),
        'w_up': nrm(ks[17], (DEPTH, D_MODEL, D_FF), D_MODEL),
        'w_down': nrm(ks[18], (DEPTH, D_FF, D_MODEL), D_FF),
        'norm_ple': gain(ks[19], (DEPTH, D_MODEL)),
        'w_ple_gate': nrm(ks[20], (DEPTH, D_MODEL, D_MODEL), D_MODEL),
        'w_ple_proj': nrm(ks[21], (DEPTH, PLE_DIM, D_MODEL), PLE_DIM),
        'norm_final': gain(ks[22], (D_MODEL,)),
    }


def reference(x, p, norm_mix, w_in, gdn_conv_w, gdn_A_log, gdn_dt_bias, gdn_norm_w,
              ssm_conv_w, ssm_conv_b, ssm_A_log, ssm_dt_bias, ssm_D, ssm_norm_w,
              w_branch, w_out, norm_mlp, w_up, w_down, norm_ple, w_ple_gate, w_ple_proj,
              norm_final):
    Bz, S, D = x.shape
    for i in range(DEPTH):
        h = rmsnorm(x, norm_mix[i])
        proj = jnp.einsum('bsd,dn->bsn', h, w_in[i])
        (a_qkv, a_z, a_beta, a_alpha, b_z, b_xbc, b_dt,
         c_q, c_k, c_v, c_g, gate_logits) = split_cols(proj, IN_SPLITS)
        y_a = gated_deltanet(a_qkv, a_z, a_beta, a_alpha, gdn_conv_w[i], gdn_A_log[i],
                             gdn_dt_bias[i], gdn_norm_w[i])
        y_b = mamba2_ssd(b_z, b_xbc, b_dt, ssm_conv_w[i], ssm_conv_b[i], ssm_A_log[i],
                         ssm_dt_bias[i], ssm_D[i], ssm_norm_w[i])
        y_c = retention(c_q, c_k, c_v, c_g)
        ys = jnp.stack([y_a, y_b, y_c], axis=2).astype(x.dtype)
        branch = jnp.einsum('bsnc,ncd->bsnd', ys, w_branch[i])
        gates = jax.nn.sigmoid(gate_logits.reshape(Bz, S, N_BRANCH, D))
        mixed = jnp.sum(gates * branch, axis=2)
        x = x + jnp.einsum('bsd,de->bse', mixed, w_out[i])
        h = rmsnorm(x, norm_mlp[i])
        u = jax.nn.relu(jnp.einsum('bsd,df->bsf', h, w_up[i]))
        x = x + jnp.einsum('bsf,fd->bsd', u * u, w_down[i])
        h = rmsnorm(x, norm_ple[i])
        ple = jnp.einsum('bse,ed->bsd', p[i], w_ple_proj[i])
        x = x + jax.nn.sigmoid(jnp.einsum('bsd,de->bse', h, w_ple_gate[i])) * ple
    return rmsnorm(x, norm_final)
```

```python
import functools
import math

import numpy as np
import jax
import jax.numpy as jnp
from jax import lax
from jax.experimental import pallas as pl
from jax.experimental.pallas import tpu as pltpu

F32 = jnp.float32
BF16 = jnp.bfloat16

D_MODEL = 1024
PLE_DIM = 256
D_FF = 4 * D_MODEL
NORM_EPS = 1e-6
N_HEADS = 8
HEAD_W = 64
BRANCH_W = N_HEADS * HEAD_W
QUAD_W = 4 * HEAD_W
SMALL_W = 128
GDN_CHUNK = 64
SSM_CHUNK = 128
RET_CHUNK = 128
SSM_GROUPS = 2
SSM_STATE = 64
ROPE_BASE = 10000.0
CONV_K = 4
CONV_PAD = 8

TILE_GDN = 256
TILE_SSD = 128
TILE_RET = 128
TILE_POST = 512
FF_BLOCK = 1024
VMEM_LIMIT_MIXER = 40 * 1024 * 1024
VMEM_LIMIT_POST = 56 * 1024 * 1024


def _rmsnorm_rows(x, w):
    y = x * lax.rsqrt(jnp.mean(x * x, axis=-1, keepdims=True) + NORM_EPS)
    return y * w


def _sigmoid(x):
    return 1.0 / (1.0 + jnp.exp(-x))


def _silu(x):
    return x * _sigmoid(x)


def _softplus(x):
    return jnp.maximum(x, 0.0) + jnp.log1p(jnp.exp(-jnp.abs(x)))


def _dot(a, b):
    return jnp.dot(a.astype(BF16), b.astype(BF16), preferred_element_type=F32)


def _dot_nt(a, b):
    return lax.dot_general(a.astype(BF16), b.astype(BF16), (((1,), (1,)), ((), ())),
                           preferred_element_type=F32)


def _dot_tn(a, b):
    return lax.dot_general(a.astype(BF16), b.astype(BF16), (((0,), (0,)), ((), ())),
                           preferred_element_type=F32)


def _split3(a):
    a1 = a.astype(BF16)
    r = a - a1.astype(F32)
    a2 = r.astype(BF16)
    a3 = (r - a2.astype(F32)).astype(BF16)
    return a1, a2, a3


def _dot_exact_rhs(a_bf16_exact, b):
    b1, b2, b3 = _split3(b)
    out = jnp.dot(a_bf16_exact, b1, preferred_element_type=F32)
    out = out + jnp.dot(a_bf16_exact, b2, preferred_element_type=F32)
    return out + jnp.dot(a_bf16_exact, b3, preferred_element_type=F32)


def _dot_exact_lhs(a, b_bf16_exact):
    a1 = a.astype(BF16)
    a2 = (a - a1.astype(F32)).astype(BF16)
    return (jnp.dot(a1, b_bf16_exact, preferred_element_type=F32)
            + jnp.dot(a2, b_bf16_exact, preferred_element_type=F32))


def _iota(shape, dim):
    return lax.broadcasted_iota(jnp.int32, shape, dim)


def _block_diag_rows(y):
    head = _iota(y.shape, 1) // HEAD_W
    yb = y.astype(BF16)
    zero = jnp.zeros_like(yb)
    return jnp.concatenate([jnp.where(head == e, yb, zero) for e in range(4)], axis=0)


def _expand_heads(cols, first_lane, rows):
    low = _iota((rows, 128), 1) < HEAD_W
    parts = []
    for p in range(N_HEADS // 2):
        a = jnp.broadcast_to(cols[:, first_lane + 2 * p:first_lane + 2 * p + 1], (rows, 128))
        b = jnp.broadcast_to(cols[:, first_lane + 2 * p + 1:first_lane + 2 * p + 2], (rows, 128))
        parts.append(jnp.where(low, a, b))
    return jnp.concatenate(parts, axis=1)


def _causal_conv(pre, cbuf_ref, w_ref, rows):
    cbuf_ref[CONV_PAD:CONV_PAD + rows, :] = pre
    acc = pre * w_ref[CONV_K - 1:CONV_K, :]
    for k in range(CONV_K - 1):
        off = CONV_PAD - (CONV_K - 1) + k
        acc = acc + cbuf_ref[off:off + rows, :] * w_ref[k:k + 1, :]
    cbuf_ref[0:CONV_PAD, :] = pre[rows - CONV_PAD:rows, :]
    return acc


def _head_rms(o, ones_bd_ref):
    return _dot_exact_lhs(o * o, ones_bd_ref[...]) * (1.0 / HEAD_W)


def _inv_unit_lower_quad(a):
    r = _iota(a.shape, 0)
    c = _iota(a.shape, 1) % HEAD_W
    eye = jnp.where(r == c, 1.0, 0.0).astype(F32)
    in16 = (r // 16) == (c // 16)
    in32 = (r // 32) == (c // 32)

    def mm(x, y):
        return jnp.dot(x.astype(BF16), _block_diag_rows(y), preferred_element_type=F32)

    n = jnp.where(in16, -a, 0.0)
    x = eye + n
    p = mm(n, n)
    x = x + mm(x, p)
    p = mm(p, p)
    x = x + mm(x, p)
    p = mm(p, p)
    x = x + mm(x, p)
    l1 = jnp.where(jnp.logical_and(in32, jnp.logical_not(in16)), a, 0.0)
    x = x - mm(mm(x, l1), x)
    l2 = jnp.where(in32, 0.0, a)
    x = x - mm(mm(x, l2), x)
    return x


def _gdn_kernel(x_ref, nw_ref, wqkv_ref, wz_ref, wsm_ref, convw_ref, alog_ref, dtb_ref,
                gnw_ref, ones_bd_ref, tri_ref, o_ref, cbuf_ref, state_ref, stage_ref):
    T = TILE_GDN
    L = GDN_CHUNK

    @pl.when(pl.program_id(1) == 0)
    def _():
        state_ref[...] = jnp.zeros_like(state_ref)
        cbuf_ref[0:CONV_PAD, :] = jnp.zeros((CONV_PAD, cbuf_ref.shape[1]), F32)

    h = _rmsnorm_rows(x_ref[0], nw_ref[...]).astype(BF16)
    pre = jnp.dot(h, wqkv_ref[...], preferred_element_type=F32)
    qkv = _silu(_causal_conv(pre, cbuf_ref, convw_ref, T))
    q = qkv[:, 0:BRANCH_W]
    k = qkv[:, BRANCH_W:2 * BRANCH_W]
    v = qkv[:, 2 * BRANCH_W:3 * BRANCH_W]
    ones_bd = ones_bd_ref[...]
    q = q * lax.rsqrt(_dot_exact_lhs(q * q, ones_bd) + 1e-6) * (HEAD_W ** -0.5)
    k = k * lax.rsqrt(_dot_exact_lhs(k * k, ones_bd) + 1e-6)

    small = jnp.dot(h, wsm_ref[...], preferred_element_type=F32)
    beta = _sigmoid(small)
    glog = -jnp.exp(alog_ref[...]) * _softplus(small + dtb_ref[...])
    gcum = _dot_exact_rhs(tri_ref[...], glog)
    beta_x = _expand_heads(beta, 0, T)
    g_x = _expand_heads(gcum, N_HEADS, T)

    shape = (L, QUAD_W)
    r = _iota(shape, 0)
    c = _iota(shape, 1) % HEAD_W
    diag = r == c
    lower = r >= c
    strict = r > c
    rr = _iota((QUAD_W, QUAD_W), 0) // HEAD_W
    cc = _iota((QUAD_W, QUAD_W), 1) // HEAD_W
    bd_mask = rr == cc

    for ci in range(T // L):
        r0 = ci * L
        for u in range(2):
            l0 = u * QUAD_W
            kq = k[r0:r0 + L, l0:l0 + QUAD_W]
            qq = q[r0:r0 + L, l0:l0 + QUAD_W]
            vq = v[r0:r0 + L, l0:l0 + QUAD_W]
            be = beta_x[r0:r0 + L, l0:l0 + QUAD_W]
            gq = g_x[r0:r0 + L, l0:l0 + QUAD_W]
            kb = kq * be
            vb = vq * be
            eg = jnp.exp(gq)
            g_row = jnp.sum(jnp.where(diag, gq, 0.0), axis=0, keepdims=True)
            dec = jnp.exp(jnp.where(lower, gq - g_row, -jnp.inf))
            kexp = _block_diag_rows(kq)
            sc = _dot_nt(jnp.concatenate([kb, qq], axis=0), kexp)
            a = jnp.where(strict, sc[0:L] * dec, 0.0)
            attn = sc[L:2 * L] * dec
            tinv = _inv_unit_lower_quad(a)
            rhs = jnp.concatenate([_block_diag_rows(vb), _block_diag_rows(kb * eg)], axis=1)
            uw = jnp.dot(tinv.astype(BF16), rhs, preferred_element_type=F32)
            s_old = state_ref[u]
            wq = _dot(jnp.concatenate([uw[:, QUAD_W:], qq * eg], axis=0), s_old)
            v_new = uw[:, 0:QUAD_W] - wq[0:L]
            o = wq[L:2 * L] + jnp.dot(attn.astype(BF16), _block_diag_rows(v_new),
                                      preferred_element_type=F32)
            g_last = gq[L - 1:L, :]
            kd = kq * jnp.exp(g_last - gq)
            upd = _dot_tn(kd, v_new)
            state_ref[u] = s_old * jnp.exp(g_last) + jnp.where(bd_mask, upd, 0.0)
            stage_ref[r0:r0 + L, l0:l0 + QUAD_W] = o

    o_all = stage_ref[...]
    z = jnp.dot(h, wz_ref[...], preferred_element_type=F32)
    y = o_all * lax.rsqrt(_head_rms(o_all, ones_bd_ref) + NORM_EPS) * gnw_ref[...] * _silu(z)
    o_ref[0] = y.astype(o_ref.dtype)


def _ssd_kernel(x_ref, nw_ref, wz_ref, wxbc_ref, wsm_ref, convw_ref, convb_ref, alog_ref,
                dtb_ref, dskip_ref, gnw_ref, tri_ref, o_ref, cbuf_ref, state_ref):
    T = TILE_SSD
    L = SSM_CHUNK
    assert T == L

    @pl.when(pl.program_id(1) == 0)
    def _():
        state_ref[...] = jnp.zeros_like(state_ref)
        cbuf_ref[0:CONV_PAD, :] = jnp.zeros((CONV_PAD, cbuf_ref.shape[1]), F32)

    h = _rmsnorm_rows(x_ref[0], nw_ref[...]).astype(BF16)
    pre = jnp.dot(h, wxbc_ref[...], preferred_element_type=F32)
    xbc = _silu(_causal_conv(pre, cbuf_ref, convw_ref, T) + convb_ref[...])
    xs = xbc[:, 0:BRANCH_W]
    b_pair = xbc[:, BRANCH_W:BRANCH_W + 128]
    c_pair = xbc[:, BRANCH_W + 128:BRANCH_W + 256]
    z = jnp.dot(h, wz_ref[...], preferred_element_type=F32)
    small = jnp.dot(h, wsm_ref[...], preferred_element_type=F32)
    dt = _softplus(small + dtb_ref[...])
    a = -jnp.exp(alog_ref[...]) * dt
    acs = _dot_exact_rhs(tri_ref[...], a)
    dt_x = _expand_heads(dt, 0, T)
    acs_x = _expand_heads(acs, 0, T)
    xdt = xs * dt_x

    r = _iota((L, L), 0)
    c = _iota((L, L), 1)
    diag = r == c
    lower = r >= c
    lane_low = _iota((L, 128), 1) < SSM_STATE
    acs_last_x = acs_x[L - 1:L, :]
    ys = []
    for g in range(SSM_GROUPS):
        l0 = g * QUAD_W
        keep = lane_low if g == 0 else jnp.logical_not(lane_low)
        b_g = jnp.where(keep, b_pair, 0.0)
        c_g = jnp.where(keep, c_pair, 0.0)
        cb = _dot_nt(c_g, b_pair)
        m_parts = []
        for e in range(4):
            hd = 4 * g + e
            col = jnp.broadcast_to(acs[:, hd:hd + 1], (L, L))
            row = jnp.sum(jnp.where(diag, col, 0.0), axis=0, keepdims=True)
            m_parts.append(cb * jnp.exp(jnp.where(lower, col - row, -jnp.inf)))
        m_quad = jnp.concatenate(m_parts, axis=1)
        xq = xdt[:, l0:l0 + QUAD_W]
        y_diag = jnp.dot(m_quad.astype(BF16), _block_diag_rows(xq), preferred_element_type=F32)
        acs_q = acs_x[:, l0:l0 + QUAD_W]
        last_q = acs_last_x[:, l0:l0 + QUAD_W]
        states = _dot_tn(b_g, xq * jnp.exp(last_q - acs_q))
        h_old = state_ref[g]
        y_off = _dot(c_g, h_old) * jnp.exp(acs_q)
        state_ref[g] = h_old * jnp.exp(last_q) + states
        ys.append(y_diag + y_off)
    y = jnp.concatenate(ys, axis=1) + dskip_ref[...] * xs
    y = y * _silu(z)
    outs = []
    for g in range(SSM_GROUPS):
        yg = y[:, g * QUAD_W:(g + 1) * QUAD_W]
        outs.append(yg * lax.rsqrt(jnp.mean(yg * yg, axis=-1, keepdims=True) + NORM_EPS))
    o_ref[0] = (jnp.concatenate(outs, axis=1) * gnw_ref[...]).astype(o_ref.dtype)


def _rotate_pairs(t, cos, sin_signed):
    n = t.shape[1]
    lo = (_iota(t.shape, 1) % HEAD_W) < (HEAD_W // 2)
    partner = jnp.where(lo, pltpu.roll(t, n - HEAD_W // 2, 1), pltpu.roll(t, HEAD_W // 2, 1))
    return t * cos + partner * sin_signed


def _ret_kernel(x_ref, nw_ref, w_ref, cos_ref, sin_ref, dmat_ref, qdec_ref, kdec_ref,
                cgam_ref, ones_bd_ref, o_ref, state_ref):
    T = TILE_RET
    L = RET_CHUNK
    assert T == L

    @pl.when(pl.program_id(1) == 0)
    def _():
        state_ref[...] = jnp.zeros_like(state_ref)

    h = _rmsnorm_rows(x_ref[0], nw_ref[...]).astype(BF16)
    proj = jnp.dot(h, w_ref[...], preferred_element_type=F32)
    cos = cos_ref[...]
    sin = sin_ref[...]
    q = _rotate_pairs(proj[:, 0:BRANCH_W], cos, sin)
    k = _rotate_pairs(proj[:, BRANCH_W:2 * BRANCH_W], cos, sin) * (HEAD_W ** -0.5)
    v = proj[:, 2 * BRANCH_W:3 * BRANCH_W]
    gate = proj[:, 3 * BRANCH_W:4 * BRANCH_W]
    rr = _iota((QUAD_W, QUAD_W), 0) // HEAD_W
    cc = _iota((QUAD_W, QUAD_W), 1) // HEAD_W
    bd_mask = rr == cc
    ys = []
    for u in range(2):
        l0 = u * QUAD_W
        qq = q[:, l0:l0 + QUAD_W]
        kq = k[:, l0:l0 + QUAD_W]
        vq = v[:, l0:l0 + QUAD_W]
        sc = _dot_nt(qq, _block_diag_rows(kq))
        attn = sc * dmat_ref[u]
        y_in = jnp.dot(attn.astype(BF16), _block_diag_rows(vq), preferred_element_type=F32)
        s_old = state_ref[u]
        y_off = _dot(qq * qdec_ref[:, l0:l0 + QUAD_W], s_old)
        kv = _dot_tn(kq * kdec_ref[:, l0:l0 + QUAD_W], vq)
        state_ref[u] = s_old * cgam_ref[:, l0:l0 + QUAD_W] + jnp.where(bd_mask, kv, 0.0)
        ys.append(y_in + y_off)
    y = jnp.concatenate(ys, axis=1)
    y = y * lax.rsqrt(_head_rms(y, ones_bd_ref) + NORM_EPS) * _silu(gate)
    o_ref[0] = y.astype(o_ref.dtype)


def _post_kernel(x_ref, ya_ref, yb_ref, yc_ref, p_ref, nmix_ref, wgate_ref, wbr_ref, wout_ref,
                 nmlp_ref, wup_ref, wdown_ref, nple_ref, wpg_ref, wpp_ref, nfin_ref, o_ref,
                 *, final_norm):
    x = x_ref[...]
    h = _rmsnorm_rows(x, nmix_ref[...]).astype(BF16)
    mixed = None
    for n, y_ref in enumerate((ya_ref, yb_ref, yc_ref)):
        logits = jnp.dot(h, wgate_ref[:, n * D_MODEL:(n + 1) * D_MODEL], preferred_element_type=F32)
        branch = jnp.dot(y_ref[...], wbr_ref[n], preferred_element_type=F32)
        term = _sigmoid(logits) * branch
        mixed = term if mixed is None else mixed + term
    x = x + jnp.dot(mixed.astype(BF16), wout_ref[...], preferred_element_type=F32)

    h = _rmsnorm_rows(x, nmlp_ref[...]).astype(BF16)
    acc = None
    for f in range(D_FF // FF_BLOCK):
        u = jnp.maximum(jnp.dot(h, wup_ref[:, f * FF_BLOCK:(f + 1) * FF_BLOCK],
                                preferred_element_type=F32), 0.0)
        d = jnp.dot((u * u).astype(BF16), wdown_ref[f * FF_BLOCK:(f + 1) * FF_BLOCK, :],
                    preferred_element_type=F32)
        acc = d if acc is None else acc + d
    x = x + acc

    h = _rmsnorm_rows(x, nple_ref[...]).astype(BF16)
    ple = jnp.dot(p_ref[...].astype(BF16), wpp_ref[...], preferred_element_type=F32)
    x = x + _sigmoid(jnp.dot(h, wpg_ref[...], preferred_element_type=F32)) * ple
    if final_norm:
        x = _rmsnorm_rows(x, nfin_ref[...])
    o_ref[...] = x


def _const_spec(shape):
    zeros = (0,) * len(shape)
    return pl.BlockSpec(shape, lambda *_: zeros, pipeline_mode=pl.Buffered(1))


def _ones_block_diag():
    i = np.arange(BRANCH_W) // HEAD_W
    return jnp.asarray((i[:, None] == i[None, :]).astype(np.float32), dtype=BF16)


def _chunk_tri(rows, chunk):
    i = np.arange(rows)
    m = (i[:, None] >= i[None, :]) & ((i[:, None] // chunk) == (i[None, :] // chunk))
    return jnp.asarray(m.astype(np.float32), dtype=BF16)


def _lane_vec(v, first_lane):
    out = jnp.zeros((1, SMALL_W), F32)
    return out.at[0, first_lane:first_lane + v.shape[0]].set(v.astype(F32))


def _pad_cols(w, first_lane):
    out = jnp.zeros((w.shape[0], SMALL_W), w.dtype)
    return out.at[:, first_lane:first_lane + w.shape[1]].set(w)


def _mixer_call(body, name, tile, x, consts, scratch, vmem_limit):
    B, S, D = x.shape
    in_specs = [pl.BlockSpec((1, tile, D), lambda b, j: (b, j, 0))]
    in_specs += [_const_spec(c.shape) for c in consts]
    return pl.pallas_call(
        body,
        out_shape=jax.ShapeDtypeStruct((B, S, BRANCH_W), BF16),
        grid=(B, S // tile),
        in_specs=in_specs,
        out_specs=pl.BlockSpec((1, tile, BRANCH_W), lambda b, j: (b, j, 0)),
        scratch_shapes=scratch,
        compiler_params=pltpu.CompilerParams(
            dimension_semantics=("arbitrary", "arbitrary"), vmem_limit_bytes=vmem_limit),
        name=name,
    )(x, *consts)


def _retention_tables(S):
    half = HEAD_W // 2
    inv_freq = ROPE_BASE ** (-jnp.linspace(0.0, 1.0, half, dtype=F32))
    ang = jnp.arange(S, dtype=F32)[:, None] * inv_freq[None, :]
    cos = jnp.tile(jnp.cos(ang), (1, 2 * N_HEADS))
    sin = jnp.sin(ang)
    sin_signed = jnp.tile(jnp.concatenate([-sin, sin], axis=1), (1, N_HEADS))
    L = RET_CHUNK
    log_gamma = jnp.log1p(-jnp.exp2(-5.0 - jnp.arange(N_HEADS, dtype=F32)))
    idx = jnp.arange(L, dtype=F32)
    tril = jnp.tril(jnp.ones((L, L), bool))
    dmat = jnp.exp(jnp.where(tril, (idx[:, None] - idx[None, :]) * log_gamma[:, None, None], -jnp.inf))
    dmat = dmat.reshape(2, 4, L, L).transpose(0, 2, 1, 3).reshape(2, L, 4 * L)
    kdec = jnp.repeat(jnp.exp((L - 1 - idx)[:, None] * log_gamma[None, :]), HEAD_W, axis=1)
    qdec = jnp.repeat(jnp.exp((idx + 1)[:, None] * log_gamma[None, :]), HEAD_W, axis=1)
    cgam = jnp.repeat(jnp.exp(L * log_gamma)[None, :], HEAD_W, axis=1)
    return cos, sin_signed, dmat, qdec, kdec, cgam


def kernel(x, p, norm_mix, w_in, gdn_conv_w, gdn_A_log, gdn_dt_bias, gdn_norm_w, ssm_conv_w,
           ssm_conv_b, ssm_A_log, ssm_dt_bias, ssm_D, ssm_norm_w, w_branch, w_out, norm_mlp,
           w_up, w_down, norm_ple, w_ple_gate, w_ple_proj, norm_final):
    B, S, D = x.shape
    depth = w_in.shape[0]
    assert D == D_MODEL and S % TILE_GDN == 0 and (B * S) % TILE_POST == 0
    ones_bd = _ones_block_diag()
    tri_gdn = _chunk_tri(TILE_GDN, GDN_CHUNK)
    tri_ssd = _chunk_tri(TILE_SSD, SSM_CHUNK)
    cos, sin_signed, dmat, qdec, kdec, cgam = _retention_tables(S)
    row = lambda v: v.reshape(1, -1).astype(F32)

    sizes = (3 * BRANCH_W, BRANCH_W, N_HEADS, N_HEADS, BRANCH_W, BRANCH_W + 2 * SSM_GROUPS * SSM_STATE,
             N_HEADS, BRANCH_W, BRANCH_W, BRANCH_W, BRANCH_W, 3 * D_MODEL)
    offs = np.concatenate([[0], np.cumsum(sizes)])
    seg = lambda w, i: w[:, offs[i]:offs[i + 1]]

    for i in range(depth):
        w = w_in[i].astype(BF16)
        nmix = row(norm_mix[i])
        consts = (nmix, seg(w, 0), seg(w, 1),
                  _pad_cols(jnp.concatenate([seg(w, 2), seg(w, 3)], axis=1), 0),
                  gdn_conv_w[i].astype(F32), _lane_vec(gdn_A_log[i], N_HEADS), _lane_vec(gdn_dt_bias[i], N_HEADS),
                  row(jnp.tile(gdn_norm_w[i], N_HEADS)), ones_bd, tri_gdn)
        y_a = _mixer_call(
            _gdn_kernel, "gdn_mixer", TILE_GDN, x, consts,
            [pltpu.VMEM((TILE_GDN + CONV_PAD, 3 * BRANCH_W), F32),
             pltpu.VMEM((2, QUAD_W, QUAD_W), F32),
             pltpu.VMEM((TILE_GDN, BRANCH_W), F32)], VMEM_LIMIT_MIXER)
        consts = (nmix, seg(w, 4), seg(w, 5), _pad_cols(seg(w, 6), 0),
                  ssm_conv_w[i].astype(F32), row(ssm_conv_b[i]), _lane_vec(ssm_A_log[i], 0),
                  _lane_vec(ssm_dt_bias[i], 0), row(jnp.repeat(ssm_D[i], HEAD_W)), row(ssm_norm_w[i]), tri_ssd)
        y_b = _mixer_call(
            _ssd_kernel, "ssd_mixer", TILE_SSD, x, consts,
            [pltpu.VMEM((TILE_SSD + CONV_PAD, BRANCH_W + 2 * SSM_GROUPS * SSM_STATE), F32),
             pltpu.VMEM((SSM_GROUPS, 128, QUAD_W), F32)], VMEM_LIMIT_MIXER)
        w_c = w[:, offs[7]:offs[11]]
        B_, S_, _ = x.shape
        in_specs = [pl.BlockSpec((1, TILE_RET, D), lambda b, j: (b, j, 0)),
                    _const_spec(nmix.shape), _const_spec(w_c.shape),
                    pl.BlockSpec((TILE_RET, BRANCH_W), lambda b, j: (j, 0)),
                    pl.BlockSpec((TILE_RET, BRANCH_W), lambda b, j: (j, 0)),
                    _const_spec(dmat.shape), _const_spec(qdec.shape), _const_spec(kdec.shape),
                    _const_spec(cgam.shape), _const_spec(ones_bd.shape)]
        y_c = pl.pallas_call(
            _ret_kernel,
            out_shape=jax.ShapeDtypeStruct((B, S, BRANCH_W), BF16),
            grid=(B, S // TILE_RET),
            in_specs=in_specs,
            out_specs=pl.BlockSpec((1, TILE_RET, BRANCH_W), lambda b, j: (b, j, 0)),
            scratch_shapes=[pltpu.VMEM((2, QUAD_W, QUAD_W), F32)],
            compiler_params=pltpu.CompilerParams(
                dimension_semantics=("arbitrary", "arbitrary"), vmem_limit_bytes=VMEM_LIMIT_MIXER),
            name="ret_mixer",
        )(x, nmix, w_c, cos, sin_signed, dmat, qdec, kdec, cgam, ones_bd)
        n_tok = B * S
        tok = lambda a: a.reshape(n_tok, a.shape[-1])
        consts = (nmix, seg(w, 11), w_branch[i].astype(BF16), w_out[i].astype(BF16), row(norm_mlp[i]),
                  w_up[i].astype(BF16), w_down[i].astype(BF16), row(norm_ple[i]),
                  w_ple_gate[i].astype(BF16), w_ple_proj[i].astype(BF16), row(norm_final))
        tile_spec = lambda width: pl.BlockSpec((TILE_POST, width), lambda t: (t, 0))
        x = pl.pallas_call(
            functools.partial(_post_kernel, final_norm=(i == depth - 1)),
            out_shape=jax.ShapeDtypeStruct((n_tok, D), F32),
            grid=(n_tok // TILE_POST,),
            in_specs=[tile_spec(D), tile_spec(BRANCH_W), tile_spec(BRANCH_W), tile_spec(BRANCH_W),
                      tile_spec(PLE_DIM)] + [_const_spec(c.shape) for c in consts],
            out_specs=tile_spec(D),
            compiler_params=pltpu.CompilerParams(
                dimension_semantics=("arbitrary",), vmem_limit_bytes=VMEM_LIMIT_POST),
            name="merge_mlp_ple",
        )(tok(x), tok(y_a), tok(y_b), tok(y_c), tok(p[i]), *consts).reshape(B, S, D)
    return x
```

```python
import functools
import math

import numpy as np
import jax
import jax.numpy as jnp
from jax import lax
from jax.experimental import pallas as pl
from jax.experimental.pallas import tpu as pltpu

F32 = jnp.float32
BF16 = jnp.bfloat16

D_MODEL = 1024
PLE_DIM = 256
D_FF = 4 * D_MODEL
NORM_EPS = 1e-6
N_HEADS = 8
HEAD_W = 64
BRANCH_W = N_HEADS * HEAD_W
QUAD_W = 4 * HEAD_W
SMALL_W = 128
GDN_CHUNK = 64
SSM_CHUNK = 128
RET_CHUNK = 128
SSM_GROUPS = 2
SSM_STATE = 64
ROPE_BASE = 10000.0
CONV_K = 4
CONV_PAD = 8

TILE_GDN = 64
TILE_SSD = 128
TILE_RET = 128
TILE_POST = 512
FF_BLOCK = 1024
VMEM_LIMIT_MIXER = 40 * 1024 * 1024
VMEM_LIMIT_POST = 56 * 1024 * 1024


def _rmsnorm_rows(x, w):
    y = x * lax.rsqrt(jnp.mean(x * x, axis=-1, keepdims=True) + NORM_EPS)
    return y * w


def _sigmoid(x):
    return 1.0 / (1.0 + jnp.exp(-x))


def _silu(x):
    return x * _sigmoid(x)


def _softplus(x):
    return jnp.maximum(x, 0.0) + jnp.log1p(jnp.exp(-jnp.abs(x)))


def _dot(a, b):
    return jnp.dot(a.astype(BF16), b.astype(BF16), preferred_element_type=F32)


def _dot_nt(a, b):
    return lax.dot_general(a.astype(BF16), b.astype(BF16), (((1,), (1,)), ((), ())),
                           preferred_element_type=F32)


def _dot_tn(a, b):
    return lax.dot_general(a.astype(BF16), b.astype(BF16), (((0,), (0,)), ((), ())),
                           preferred_element_type=F32)


def _chunk_cumsum(x, tri_ref):
    x1 = x.astype(BF16)
    x2 = (x - x1.astype(F32)).astype(BF16)
    tri = tri_ref[...]
    outs = []
    for s in range(x.shape[0] // 128):
        sl = slice(s * 128, (s + 1) * 128)
        outs.append(jnp.dot(tri, x1[sl], preferred_element_type=F32)
                    + jnp.dot(tri, x2[sl], preferred_element_type=F32))
    return jnp.concatenate(outs, axis=0)


def _iota(shape, dim):
    return lax.broadcasted_iota(jnp.int32, shape, dim)


def _block_diag_rows(y):
    head = _iota(y.shape, 1) // HEAD_W
    yb = y.astype(BF16)
    zero = jnp.zeros_like(yb)
    return jnp.concatenate([jnp.where(head == e, yb, zero) for e in range(4)], axis=0)


def _expand_heads(cols, first_lane, rows):
    low = _iota((rows, 128), 1) < HEAD_W
    parts = []
    for p in range(N_HEADS // 2):
        a = jnp.broadcast_to(cols[:, first_lane + 2 * p:first_lane + 2 * p + 1], (rows, 128))
        b = jnp.broadcast_to(cols[:, first_lane + 2 * p + 1:first_lane + 2 * p + 2], (rows, 128))
        parts.append(jnp.where(low, a, b))
    return jnp.concatenate(parts, axis=1)


def _causal_conv(pre, cbuf_ref, w_ref, rows):
    cbuf_ref[CONV_PAD:CONV_PAD + rows, :] = pre
    acc = pre * w_ref[CONV_K - 1:CONV_K, :]
    for k in range(CONV_K - 1):
        off = CONV_PAD - (CONV_K - 1) + k
        acc = acc + cbuf_ref[off:off + rows, :] * w_ref[k:k + 1, :]
    cbuf_ref[0:CONV_PAD, :] = pre[rows - CONV_PAD:rows, :]
    return acc


def _head_sum(sq, ones_bd_ref):
    return jnp.dot(sq.astype(BF16), ones_bd_ref[...], preferred_element_type=F32)


def _inv_unit_lower_quads(mats):
    shape = mats[0].shape
    r = _iota(shape, 0)
    c = _iota(shape, 1) % HEAD_W
    eye = jnp.where(r == c, 1.0, 0.0).astype(F32)
    in16 = (r // 16) == (c // 16)
    in32 = (r // 32) == (c // 32)
    off16 = jnp.logical_and(in32, jnp.logical_not(in16))

    def mm(xs, ys):
        return [jnp.dot(x.astype(BF16), _block_diag_rows(y), preferred_element_type=F32)
                for x, y in zip(xs, ys)]

    def add(xs, ys):
        return [x + y for x, y in zip(xs, ys)]

    def sub(xs, ys):
        return [x - y for x, y in zip(xs, ys)]

    ns = [jnp.where(in16, -a, 0.0) for a in mats]
    xs = [eye + n for n in ns]
    ps = mm(ns, ns)
    xs = add(xs, mm(xs, ps))
    ps = mm(ps, ps)
    xs = add(xs, mm(xs, ps))
    ps = mm(ps, ps)
    xs = add(xs, mm(xs, ps))
    l1 = [jnp.where(off16, a, 0.0) for a in mats]
    xs = sub(xs, mm(mm(xs, l1), xs))
    l2 = [jnp.where(in32, 0.0, a) for a in mats]
    xs = sub(xs, mm(mm(xs, l2), xs))
    return xs


def _gdn_kernel(x_ref, nw_ref, wqkv_ref, wz_ref, wsm_ref, convw_ref, alog_ref, dtb_ref,
                gnw_ref, ones_bd_ref, tri_ref, o_ref, cbuf_ref, state_ref, *, nb):
    T = TILE_GDN
    L = GDN_CHUNK
    n_chunks = T // L
    R = nb * T

    @pl.when(pl.program_id(0) == 0)
    def _():
        state_ref[...] = jnp.zeros_like(state_ref)
        cbuf_ref[:, 0:CONV_PAD, :] = jnp.zeros((nb, CONV_PAD, cbuf_ref.shape[2]), F32)

    h = _rmsnorm_rows(x_ref[...].reshape(R, D_MODEL), nw_ref[...]).astype(BF16)
    pre = jnp.dot(h, wqkv_ref[...], preferred_element_type=F32)
    conv = jnp.concatenate(
        [_causal_conv(pre[b * T:(b + 1) * T], cbuf_ref.at[b], convw_ref, T) for b in range(nb)], axis=0)
    qkv = _silu(conv)
    q = qkv[:, 0:BRANCH_W]
    k = qkv[:, BRANCH_W:2 * BRANCH_W]
    v = qkv[:, 2 * BRANCH_W:3 * BRANCH_W]
    q = q * lax.rsqrt(_head_sum(q * q, ones_bd_ref) + 1e-6) * (HEAD_W ** -0.5)
    k = k * lax.rsqrt(_head_sum(k * k, ones_bd_ref) + 1e-6)

    small = jnp.dot(h, wsm_ref[...], preferred_element_type=F32)
    beta = _sigmoid(small)
    glog = -jnp.exp(alog_ref[...]) * _softplus(small + dtb_ref[...])
    gcum = _chunk_cumsum(glog, tri_ref)
    beta_x = _expand_heads(beta, 0, R)
    g_x = _expand_heads(gcum, N_HEADS, R)
    kb = k * beta_x
    vb = v * beta_x
    eg = jnp.exp(g_x)
    kbg = kb * eg
    qd = q * eg

    shape = (L, QUAD_W)
    r = _iota(shape, 0)
    c = _iota(shape, 1) % HEAD_W
    diag = r == c
    lower = r >= c
    strict = r > c
    rr = _iota((QUAD_W, QUAD_W), 0) // HEAD_W
    cc = _iota((QUAD_W, QUAD_W), 1) // HEAD_W
    bd_mask = rr == cc

    def piece(t, b, ci, u):
        r0 = b * T + ci * L
        return t[r0:r0 + L, u * QUAD_W:(u + 1) * QUAD_W]

    items = [(b, ci, u) for b in range(nb) for ci in range(n_chunks) for u in range(2)]
    scs = [_dot_nt(jnp.concatenate([piece(kb, *it), piece(q, *it)], axis=0),
                   _block_diag_rows(piece(k, *it))) for it in items]
    a_list, attn_list = [], []
    for it, sc in zip(items, scs):
        gq = piece(g_x, *it)
        g_row = jnp.sum(jnp.where(diag, gq, 0.0), axis=0, keepdims=True)
        dec = jnp.exp(jnp.where(lower, gq - g_row, -jnp.inf))
        a_list.append(jnp.where(strict, sc[0:L] * dec, 0.0))
        attn_list.append(sc[L:2 * L] * dec)
    tinv_list = _inv_unit_lower_quads(a_list)
    uw = {}
    for it, tinv in zip(items, tinv_list):
        rhs = jnp.concatenate([_block_diag_rows(piece(vb, *it)),
                               _block_diag_rows(piece(kbg, *it))], axis=1)
        uw[it] = jnp.dot(tinv.astype(BF16), rhs, preferred_element_type=F32)
    attn = dict(zip(items, attn_list))

    chains = [(b, u) for b in range(nb) for u in range(2)]
    states = {ch: state_ref[ch[0], ch[1]] for ch in chains}
    outs = {}
    for ci in range(n_chunks):
        wq = {(b, u): _dot(jnp.concatenate([uw[(b, ci, u)][:, QUAD_W:], piece(qd, b, ci, u)], axis=0),
                           states[(b, u)]) for b, u in chains}
        v_new = {(b, u): uw[(b, ci, u)][:, 0:QUAD_W] - wq[(b, u)][0:L] for b, u in chains}
        for b, u in chains:
            gq = piece(g_x, b, ci, u)
            g_last = gq[L - 1:L, :]
            upd = _dot_tn(piece(k, b, ci, u) * jnp.exp(g_last - gq), v_new[(b, u)])
            states[(b, u)] = states[(b, u)] * jnp.exp(g_last) + jnp.where(bd_mask, upd, 0.0)
        for b, u in chains:
            outs[(b, ci, u)] = wq[(b, u)][L:2 * L] + jnp.dot(
                attn[(b, ci, u)].astype(BF16), _block_diag_rows(v_new[(b, u)]),
                preferred_element_type=F32)
    for b, u in chains:
        state_ref[b, u] = states[(b, u)]

    o_all = jnp.concatenate(
        [jnp.concatenate([outs[(b, ci, 0)], outs[(b, ci, 1)]], axis=1)
         for b in range(nb) for ci in range(n_chunks)], axis=0)
    z = jnp.dot(h, wz_ref[...], preferred_element_type=F32)
    y = o_all * lax.rsqrt(_head_sum(o_all * o_all, ones_bd_ref) * (1.0 / HEAD_W) + NORM_EPS)
    y = y * gnw_ref[...] * _silu(z)
    o_ref[...] = y.reshape(nb, T, BRANCH_W).astype(o_ref.dtype)


def _ssd_kernel(x_ref, nw_ref, wz_ref, wxbc_ref, wsm_ref, convw_ref, convb_ref, alog_ref,
                dtb_ref, dskip_ref, gnw_ref, tri_ref, o_ref, cbuf_ref, state_ref, *, nb):
    T = TILE_SSD
    L = SSM_CHUNK
    assert T == L
    R = nb * T

    @pl.when(pl.program_id(0) == 0)
    def _():
        state_ref[...] = jnp.zeros_like(state_ref)
        cbuf_ref[:, 0:CONV_PAD, :] = jnp.zeros((nb, CONV_PAD, cbuf_ref.shape[2]), F32)

    h = _rmsnorm_rows(x_ref[...].reshape(R, D_MODEL), nw_ref[...]).astype(BF16)
    pre = jnp.dot(h, wxbc_ref[...], preferred_element_type=F32)
    conv = jnp.concatenate(
        [_causal_conv(pre[b * T:(b + 1) * T], cbuf_ref.at[b], convw_ref, T) for b in range(nb)], axis=0)
    xbc = _silu(conv + convb_ref[...])
    xs = xbc[:, 0:BRANCH_W]
    b_pair = xbc[:, BRANCH_W:BRANCH_W + 128]
    c_pair = xbc[:, BRANCH_W + 128:BRANCH_W + 256]
    small = jnp.dot(h, wsm_ref[...], preferred_element_type=F32)
    dt = _softplus(small + dtb_ref[...])
    a = -jnp.exp(alog_ref[...]) * dt
    acs = _chunk_cumsum(a, tri_ref)
    dt_x = _expand_heads(dt, 0, R)
    acs_x = _expand_heads(acs, 0, R)
    xdt = xs * dt_x
    eacs_x = jnp.exp(acs_x)

    r = _iota((L, L), 0)
    c = _iota((L, L), 1)
    diag = r == c
    lower = r >= c
    lane_low = _iota((L, 128), 1) < SSM_STATE

    items = [(b, g) for b in range(nb) for g in range(SSM_GROUPS)]

    def rows(t, b):
        return t[b * T:(b + 1) * T]

    def keep(t, g):
        return jnp.where(lane_low if g == 0 else jnp.logical_not(lane_low), t, 0.0)

    c_g = {(b, g): keep(rows(c_pair, b), g) for b, g in items}
    cb = {(b, g): _dot_nt(c_g[(b, g)], rows(b_pair, b)) for b, g in items}
    y_off = {(b, g): _dot(c_g[(b, g)], state_ref[b, g]) for b, g in items}
    m_quad = {}
    for b, g in items:
        parts = []
        for e in range(4):
            hd = 4 * g + e
            col = jnp.broadcast_to(rows(acs, b)[:, hd:hd + 1], (L, L))
            row = jnp.sum(jnp.where(diag, col, 0.0), axis=0, keepdims=True)
            parts.append(cb[(b, g)] * jnp.exp(jnp.where(lower, col - row, -jnp.inf)))
        m_quad[(b, g)] = jnp.concatenate(parts, axis=1).astype(BF16)
    ys = {}
    for b, g in items:
        sl = slice(g * QUAD_W, (g + 1) * QUAD_W)
        xq = rows(xdt, b)[:, sl]
        acs_q = rows(acs_x, b)[:, sl]
        last_q = acs_q[L - 1:L, :]
        y_diag = jnp.dot(m_quad[(b, g)], _block_diag_rows(xq), preferred_element_type=F32)
        states = _dot_tn(keep(rows(b_pair, b), g), xq * jnp.exp(last_q - acs_q))
        state_ref[b, g] = state_ref[b, g] * jnp.exp(last_q) + states
        ys[(b, g)] = y_diag + y_off[(b, g)] * rows(eacs_x, b)[:, sl]
    y = jnp.concatenate([jnp.concatenate([ys[(b, 0)], ys[(b, 1)]], axis=1) for b in range(nb)], axis=0)
    z = jnp.dot(h, wz_ref[...], preferred_element_type=F32)
    y = (y + dskip_ref[...] * xs) * _silu(z)
    outs = []
    for g in range(SSM_GROUPS):
        yg = y[:, g * QUAD_W:(g + 1) * QUAD_W]
        outs.append(yg * lax.rsqrt(jnp.mean(yg * yg, axis=-1, keepdims=True) + NORM_EPS))
    y = jnp.concatenate(outs, axis=1) * gnw_ref[...]
    o_ref[...] = y.reshape(nb, T, BRANCH_W).astype(o_ref.dtype)


def _rotate_pairs(t, cos, sin_signed):
    n = t.shape[1]
    lo = (_iota(t.shape, 1) % HEAD_W) < (HEAD_W // 2)
    partner = jnp.where(lo, pltpu.roll(t, n - HEAD_W // 2, 1), pltpu.roll(t, HEAD_W // 2, 1))
    return t * cos + partner * sin_signed


def _ret_kernel(x_ref, nw_ref, w_ref, cos_ref, sin_ref, dmat_ref, qdec_ref, kdec_ref,
                cgam_ref, ones_bd_ref, o_ref, state_ref, *, nb):
    T = TILE_RET
    L = RET_CHUNK
    assert T == L
    R = nb * T

    @pl.when(pl.program_id(0) == 0)
    def _():
        state_ref[...] = jnp.zeros_like(state_ref)

    h = _rmsnorm_rows(x_ref[...].reshape(R, D_MODEL), nw_ref[...]).astype(BF16)
    proj = jnp.dot(h, w_ref[...], preferred_element_type=F32)
    cos = jnp.concatenate([cos_ref[...]] * nb, axis=0)
    sin = jnp.concatenate([sin_ref[...]] * nb, axis=0)
    q = _rotate_pairs(proj[:, 0:BRANCH_W], cos, sin)
    k = _rotate_pairs(proj[:, BRANCH_W:2 * BRANCH_W], cos, sin) * (HEAD_W ** -0.5)
    v = proj[:, 2 * BRANCH_W:3 * BRANCH_W]
    gate = proj[:, 3 * BRANCH_W:4 * BRANCH_W]
    rr = _iota((QUAD_W, QUAD_W), 0) // HEAD_W
    cc = _iota((QUAD_W, QUAD_W), 1) // HEAD_W
    bd_mask = rr == cc

    items = [(b, u) for b in range(nb) for u in range(2)]

    def piece(t, b, u):
        return t[b * T:(b + 1) * T, u * QUAD_W:(u + 1) * QUAD_W]

    sc = {it: _dot_nt(piece(q, *it), _block_diag_rows(piece(k, *it))) for it in items}
    y_off = {}
    for b, u in items:
        sl = slice(u * QUAD_W, (u + 1) * QUAD_W)
        y_off[(b, u)] = _dot(piece(q, b, u) * qdec_ref[:, sl], state_ref[b, u])
    y_in = {}
    for b, u in items:
        attn = (sc[(b, u)] * dmat_ref[u]).astype(BF16)
        y_in[(b, u)] = jnp.dot(attn, _block_diag_rows(piece(v, b, u)), preferred_element_type=F32)
    for b, u in items:
        sl = slice(u * QUAD_W, (u + 1) * QUAD_W)
        kv = _dot_tn(piece(k, b, u) * kdec_ref[:, sl], piece(v, b, u))
        state_ref[b, u] = state_ref[b, u] * cgam_ref[:, sl] + jnp.where(bd_mask, kv, 0.0)
    y = jnp.concatenate(
        [jnp.concatenate([y_in[(b, 0)] + y_off[(b, 0)], y_in[(b, 1)] + y_off[(b, 1)]], axis=1)
         for b in range(nb)], axis=0)
    y = y * lax.rsqrt(_head_sum(y * y, ones_bd_ref) * (1.0 / HEAD_W) + NORM_EPS) * _silu(gate)
    o_ref[...] = y.reshape(nb, T, BRANCH_W).astype(o_ref.dtype)


def _post_kernel(x_ref, ya_ref, yb_ref, yc_ref, p_ref, nmix_ref, wgate_ref, wbr_ref, wout_ref,
                 nmlp_ref, wup_ref, wdown_ref, nple_ref, wpg_ref, wpp_ref, nfin_ref, o_ref,
                 *, final_norm):
    x = x_ref[...]
    h = _rmsnorm_rows(x, nmix_ref[...]).astype(BF16)
    mixed = None
    for n, y_ref in enumerate((ya_ref, yb_ref, yc_ref)):
        logits = jnp.dot(h, wgate_ref[:, n * D_MODEL:(n + 1) * D_MODEL], preferred_element_type=F32)
        branch = jnp.dot(y_ref[...], wbr_ref[n], preferred_element_type=F32)
        term = _sigmoid(logits) * branch
        mixed = term if mixed is None else mixed + term
    x = x + jnp.dot(mixed.astype(BF16), wout_ref[...], preferred_element_type=F32)

    h = _rmsnorm_rows(x, nmlp_ref[...]).astype(BF16)
    acc = None
    for f in range(D_FF // FF_BLOCK):
        u = jnp.maximum(jnp.dot(h, wup_ref[:, f * FF_BLOCK:(f + 1) * FF_BLOCK],
                                preferred_element_type=F32), 0.0)
        d = jnp.dot((u * u).astype(BF16), wdown_ref[f * FF_BLOCK:(f + 1) * FF_BLOCK, :],
                    preferred_element_type=F32)
        acc = d if acc is None else acc + d
    x = x + acc

    h = _rmsnorm_rows(x, nple_ref[...]).astype(BF16)
    ple = jnp.dot(p_ref[...].astype(BF16), wpp_ref[...], preferred_element_type=F32)
    x = x + _sigmoid(jnp.dot(h, wpg_ref[...], preferred_element_type=F32)) * ple
    if final_norm:
        x = _rmsnorm_rows(x, nfin_ref[...])
    o_ref[...] = x


def _const_spec(shape):
    zeros = (0,) * len(shape)
    return pl.BlockSpec(shape, lambda *_: zeros, pipeline_mode=pl.Buffered(1))


def _ones_block_diag():
    i = np.arange(BRANCH_W) // HEAD_W
    return jnp.asarray((i[:, None] == i[None, :]).astype(np.float32), dtype=BF16)


def _chunk_tri(rows, chunk):
    i = np.arange(rows)
    m = (i[:, None] >= i[None, :]) & ((i[:, None] // chunk) == (i[None, :] // chunk))
    return jnp.asarray(m.astype(np.float32), dtype=BF16)


def _lane_vec(v, first_lane):
    out = jnp.zeros((1, SMALL_W), F32)
    return out.at[0, first_lane:first_lane + v.shape[0]].set(v.astype(F32))


def _pad_cols(w, first_lane):
    out = jnp.zeros((w.shape[0], SMALL_W), w.dtype)
    return out.at[:, first_lane:first_lane + w.shape[1]].set(w)


def _mixer_call(body, name, tile, x, in_specs_extra, operands, scratch):
    B, S, D = x.shape
    return pl.pallas_call(
        functools.partial(body, nb=B),
        out_shape=jax.ShapeDtypeStruct((B, S, BRANCH_W), BF16),
        grid=(S // tile,),
        in_specs=[pl.BlockSpec((B, tile, D), lambda j: (0, j, 0))] + in_specs_extra,
        out_specs=pl.BlockSpec((B, tile, BRANCH_W), lambda j: (0, j, 0)),
        scratch_shapes=scratch,
        compiler_params=pltpu.CompilerParams(
            dimension_semantics=("arbitrary",), vmem_limit_bytes=VMEM_LIMIT_MIXER),
        name=name,
    )(x, *operands)


def _retention_tables(S):
    half = HEAD_W // 2
    inv_freq = ROPE_BASE ** (-jnp.linspace(0.0, 1.0, half, dtype=F32))
    ang = jnp.arange(S, dtype=F32)[:, None] * inv_freq[None, :]
    cos = jnp.tile(jnp.cos(ang), (1, 2 * N_HEADS))
    sin = jnp.sin(ang)
    sin_signed = jnp.tile(jnp.concatenate([-sin, sin], axis=1), (1, N_HEADS))
    L = RET_CHUNK
    log_gamma = jnp.log1p(-jnp.exp2(-5.0 - jnp.arange(N_HEADS, dtype=F32)))
    idx = jnp.arange(L, dtype=F32)
    tril = jnp.tril(jnp.ones((L, L), bool))
    dmat = jnp.exp(jnp.where(tril, (idx[:, None] - idx[None, :]) * log_gamma[:, None, None], -jnp.inf))
    dmat = dmat.reshape(2, 4, L, L).transpose(0, 2, 1, 3).reshape(2, L, 4 * L)
    kdec = jnp.repeat(jnp.exp((L - 1 - idx)[:, None] * log_gamma[None, :]), HEAD_W, axis=1)
    qdec = jnp.repeat(jnp.exp((idx + 1)[:, None] * log_gamma[None, :]), HEAD_W, axis=1)
    cgam = jnp.repeat(jnp.exp(L * log_gamma)[None, :], HEAD_W, axis=1)
    return cos, sin_signed, dmat, qdec, kdec, cgam


def kernel(x, p, norm_mix, w_in, gdn_conv_w, gdn_A_log, gdn_dt_bias, gdn_norm_w, ssm_conv_w,
           ssm_conv_b, ssm_A_log, ssm_dt_bias, ssm_D, ssm_norm_w, w_branch, w_out, norm_mlp,
           w_up, w_down, norm_ple, w_ple_gate, w_ple_proj, norm_final):
    B, S, D = x.shape
    depth = w_in.shape[0]
    assert D == D_MODEL and S % TILE_SSD == 0 and (B * TILE_GDN) % 128 == 0 and (B * S) % TILE_POST == 0
    ones_bd = _ones_block_diag()
    tri_gdn = _chunk_tri(128, GDN_CHUNK)
    tri_ssd = _chunk_tri(128, SSM_CHUNK)
    cos, sin_signed, dmat, qdec, kdec, cgam = _retention_tables(S)
    row = lambda v: v.reshape(1, -1).astype(F32)

    sizes = (3 * BRANCH_W, BRANCH_W, N_HEADS, N_HEADS, BRANCH_W, BRANCH_W + 2 * SSM_GROUPS * SSM_STATE,
             N_HEADS, BRANCH_W, BRANCH_W, BRANCH_W, BRANCH_W, 3 * D_MODEL)
    offs = np.concatenate([[0], np.cumsum(sizes)])
    seg = lambda w, i: w[:, offs[i]:offs[i + 1]]

    for i in range(depth):
        w = w_in[i].astype(BF16)
        nmix = row(norm_mix[i])
        consts = (nmix, seg(w, 0), seg(w, 1),
                  _pad_cols(jnp.concatenate([seg(w, 2), seg(w, 3)], axis=1), 0),
                  gdn_conv_w[i].astype(F32), _lane_vec(gdn_A_log[i], N_HEADS), _lane_vec(gdn_dt_bias[i], N_HEADS),
                  row(jnp.tile(gdn_norm_w[i], N_HEADS)), ones_bd, tri_gdn)
        y_a = _mixer_call(
            _gdn_kernel, "gdn_mixer", TILE_GDN, x, [_const_spec(c.shape) for c in consts], consts,
            [pltpu.VMEM((B, TILE_GDN + CONV_PAD, 3 * BRANCH_W), F32),
             pltpu.VMEM((B, 2, QUAD_W, QUAD_W), F32)])
        consts = (nmix, seg(w, 4), seg(w, 5), _pad_cols(seg(w, 6), 0),
                  ssm_conv_w[i].astype(F32), row(ssm_conv_b[i]), _lane_vec(ssm_A_log[i], 0),
                  _lane_vec(ssm_dt_bias[i], 0), row(jnp.repeat(ssm_D[i], HEAD_W)), row(ssm_norm_w[i]), tri_ssd)
        y_b = _mixer_call(
            _ssd_kernel, "ssd_mixer", TILE_SSD, x, [_const_spec(c.shape) for c in consts], consts,
            [pltpu.VMEM((B, TILE_SSD + CONV_PAD, BRANCH_W + 2 * SSM_GROUPS * SSM_STATE), F32),
             pltpu.VMEM((B, SSM_GROUPS, 128, QUAD_W), F32)])
        w_c = w[:, offs[7]:offs[11]]
        pos_spec = pl.BlockSpec((TILE_RET, BRANCH_W), lambda j: (j, 0))
        consts = (dmat, qdec, kdec, cgam, ones_bd)
        y_c = _mixer_call(
            _ret_kernel, "ret_mixer", TILE_RET, x,
            [_const_spec(nmix.shape), _const_spec(w_c.shape), pos_spec, pos_spec]
            + [_const_spec(c.shape) for c in consts],
            (nmix, w_c, cos, sin_signed) + consts,
            [pltpu.VMEM((B, 2, QUAD_W, QUAD_W), F32)])
        n_tok = B * S
        tok = lambda a: a.reshape(n_tok, a.shape[-1])
        consts = (nmix, seg(w, 11), w_branch[i].astype(BF16), w_out[i].astype(BF16), row(norm_mlp[i]),
                  w_up[i].astype(BF16), w_down[i].astype(BF16), row(norm_ple[i]),
                  w_ple_gate[i].astype(BF16), w_ple_proj[i].astype(BF16), row(norm_final))
        tile_spec = lambda width: pl.BlockSpec((TILE_POST, width), lambda t: (t, 0))
        x = pl.pallas_call(
            functools.partial(_post_kernel, final_norm=(i == depth - 1)),
            out_shape=jax.ShapeDtypeStruct((n_tok, D), F32),
            grid=(n_tok // TILE_POST,),
            in_specs=[tile_spec(D), tile_spec(BRANCH_W), tile_spec(BRANCH_W), tile_spec(BRANCH_W),
                      tile_spec(PLE_DIM)] + [_const_spec(c.shape) for c in consts],
            out_specs=tile_spec(D),
            compiler_params=pltpu.CompilerParams(
                dimension_semantics=("arbitrary",), vmem_limit_bytes=VMEM_LIMIT_POST),
            name="merge_mlp_ple",
        )(tok(x), tok(y_a), tok(y_b), tok(y_c), tok(p[i]), *consts).reshape(B, S, D)
    return x
```

```python
import functools
import math

import numpy as np
import jax
import jax.numpy as jnp
from jax import lax
from jax.experimental import pallas as pl
from jax.experimental.pallas import tpu as pltpu

F32 = jnp.float32
BF16 = jnp.bfloat16

D_MODEL = 1024
PLE_DIM = 256
D_FF = 4 * D_MODEL
NORM_EPS = 1e-6
N_HEADS = 8
HEAD_W = 64
BRANCH_W = N_HEADS * HEAD_W
QUAD_W = 4 * HEAD_W
SMALL_W = 128
GDN_CHUNK = 64
SSM_CHUNK = 128
RET_CHUNK = 128
SSM_GROUPS = 2
SSM_STATE = 64
ROPE_BASE = 10000.0
CONV_K = 4
CONV_PAD = 8

TILE_MIX = 128
MIX_ROWS = 4
TILE_POST = 512
FF_BLOCK = 1024
VMEM_LIMIT_MIXER = 56 * 1024 * 1024
VMEM_LIMIT_POST = 56 * 1024 * 1024


def _rmsnorm_rows(x, w):
    y = x * lax.rsqrt(jnp.mean(x * x, axis=-1, keepdims=True) + NORM_EPS)
    return y * w


def _sigmoid(x):
    return 1.0 / (1.0 + jnp.exp(-x))


def _silu(x):
    return x * _sigmoid(x)


def _softplus(x):
    return jnp.maximum(x, 0.0) + jnp.log1p(jnp.exp(-jnp.abs(x)))


def _dot(a, b):
    return jnp.dot(a.astype(BF16), b.astype(BF16), preferred_element_type=F32)


def _dot_nt(a, b):
    return lax.dot_general(a.astype(BF16), b.astype(BF16), (((1,), (1,)), ((), ())),
                           preferred_element_type=F32)


def _dot_tn(a, b):
    return lax.dot_general(a.astype(BF16), b.astype(BF16), (((0,), (0,)), ((), ())),
                           preferred_element_type=F32)


def _chunk_cumsum(x, tri_ref):
    x1 = x.astype(BF16)
    x2 = (x - x1.astype(F32)).astype(BF16)
    tri = tri_ref[...]
    outs = []
    for s in range(x.shape[0] // 128):
        sl = slice(s * 128, (s + 1) * 128)
        outs.append(jnp.dot(tri, x1[sl], preferred_element_type=F32)
                    + jnp.dot(tri, x2[sl], preferred_element_type=F32))
    return jnp.concatenate(outs, axis=0)


def _iota(shape, dim):
    return lax.broadcasted_iota(jnp.int32, shape, dim)


def _block_diag_rows(y):
    head = _iota(y.shape, 1) // HEAD_W
    yb = y.astype(BF16)
    zero = jnp.zeros_like(yb)
    return jnp.concatenate([jnp.where(head == e, yb, zero) for e in range(4)], axis=0)


def _expand_heads(cols, first_lane, rows):
    low = _iota((rows, 128), 1) < HEAD_W
    parts = []
    for p in range(N_HEADS // 2):
        a = jnp.broadcast_to(cols[:, first_lane + 2 * p:first_lane + 2 * p + 1], (rows, 128))
        b = jnp.broadcast_to(cols[:, first_lane + 2 * p + 1:first_lane + 2 * p + 2], (rows, 128))
        parts.append(jnp.where(low, a, b))
    return jnp.concatenate(parts, axis=1)


def _causal_conv(pre, carry_ref, w_ref, rows):
    cat = jnp.concatenate([carry_ref[...], pre], axis=0)
    acc = pre * w_ref[CONV_K - 1:CONV_K, :]
    for s in range(1, CONV_K):
        shifted = pltpu.roll(cat, s, 0)[CONV_PAD:CONV_PAD + rows, :]
        acc = acc + shifted * w_ref[CONV_K - 1 - s:CONV_K - s, :]
    carry_ref[...] = pre[rows - CONV_PAD:rows, :]
    return acc


def _fold_block_diag(m):
    head = _iota((HEAD_W, QUAD_W), 1) // HEAD_W
    out = jnp.where(head == 0, m[0:HEAD_W], 0.0)
    for e in range(1, 4):
        out = out + jnp.where(head == e, m[e * HEAD_W:(e + 1) * HEAD_W], 0.0)
    return out


def _head_sum(sq, ones_bd_ref):
    return jnp.dot(sq.astype(BF16), ones_bd_ref[...], preferred_element_type=F32)


def _inv_unit_lower_quads(mats):
    shape = mats[0].shape
    r = _iota(shape, 0)
    c = _iota(shape, 1) % HEAD_W
    eye = jnp.where(r == c, 1.0, 0.0).astype(F32)
    in16 = (r // 16) == (c // 16)
    in32 = (r // 32) == (c // 32)
    off16 = jnp.logical_and(in32, jnp.logical_not(in16))

    def mm(xs, ys):
        return [jnp.dot(x.astype(BF16), _block_diag_rows(y), preferred_element_type=F32)
                for x, y in zip(xs, ys)]

    def add(xs, ys):
        return [x + y for x, y in zip(xs, ys)]

    def sub(xs, ys):
        return [x - y for x, y in zip(xs, ys)]

    ns = [jnp.where(in16, -a, 0.0) for a in mats]
    xs = [eye + n for n in ns]
    ps = mm(ns, ns)
    yield
    for _ in range(2):
        xs = add(xs, mm(xs, ps))
        yield
        ps = mm(ps, ps)
        yield
    xs = add(xs, mm(xs, ps))
    yield
    for keep in (off16, jnp.logical_not(in32)):
        ls = [jnp.where(keep, a, 0.0) for a in mats]
        ts = mm(xs, ls)
        yield
        xs = sub(xs, mm(ts, xs))
        yield
    return xs


def _gdn_pieces(h, wqkv_ref, wz_ref, wsm_ref, convw_ref, alog_ref, dtb_ref, gnw_ref,
                ones_bd_ref, tri_ref, o_ref, cbuf_ref, state_ref, nb):
    T = TILE_MIX
    L = GDN_CHUNK
    n_chunks = T // L
    R = nb * T
    W = BRANCH_W

    def project(gi):
        return jnp.dot(h, wqkv_ref[:, gi * W:(gi + 1) * W], preferred_element_type=F32)

    def conv_silu(pre, gi):
        conv = jnp.concatenate(
            [_causal_conv(pre[b * T:(b + 1) * T], cbuf_ref.at[b, :, gi * W:(gi + 1) * W],
                          convw_ref.at[:, gi * W:(gi + 1) * W], T) for b in range(nb)], axis=0)
        return _silu(conv)

    pre_q = project(0)
    yield
    pre_k = project(1)
    yield
    q = conv_silu(pre_q, 0)
    yield
    pre_v = project(2)
    yield
    k = conv_silu(pre_k, 1)
    yield
    z = jnp.dot(h, wz_ref[...], preferred_element_type=F32)
    yield
    v = conv_silu(pre_v, 2)
    yield
    q = q * lax.rsqrt(_head_sum(q * q, ones_bd_ref) + 1e-6) * (HEAD_W ** -0.5)
    yield
    k = k * lax.rsqrt(_head_sum(k * k, ones_bd_ref) + 1e-6)
    yield

    small = jnp.dot(h, wsm_ref[...], preferred_element_type=F32)
    beta = _sigmoid(small)
    glog = -jnp.exp(alog_ref[...]) * _softplus(small + dtb_ref[...])
    gcum = _chunk_cumsum(glog, tri_ref)
    beta_x = _expand_heads(beta, 0, R)
    g_x = _expand_heads(gcum, N_HEADS, R)
    yield
    kb = k * beta_x
    vb = v * beta_x
    eg = jnp.exp(g_x)
    kbg = kb * eg
    qd = q * eg
    yield

    shape = (L, QUAD_W)
    r = _iota(shape, 0)
    c = _iota(shape, 1) % HEAD_W
    diag = r == c
    lower = r >= c
    strict = r > c

    def piece(t, b, ci, u):
        r0 = b * T + ci * L
        return t[r0:r0 + L, u * QUAD_W:(u + 1) * QUAD_W]

    items = [(b, ci, u) for ci in range(n_chunks) for b in range(nb) for u in range(2)]
    a_mat, attn = {}, {}
    for ci in range(n_chunks):
        its = [it for it in items if it[1] == ci]
        scs = [_dot_nt(jnp.concatenate([piece(kb, *it), piece(q, *it)], axis=0),
                       _block_diag_rows(piece(k, *it))) for it in its]
        for it, sc in zip(its, scs):
            gq = piece(g_x, *it)
            g_row = jnp.sum(jnp.where(diag, gq, 0.0), axis=0, keepdims=True)
            dec = jnp.exp(jnp.where(lower, gq - g_row, -jnp.inf))
            a_mat[it] = jnp.where(strict, sc[0:L] * dec, 0.0)
            attn[it] = (sc[L:2 * L] * dec).astype(BF16)
        yield
    tinv = yield from _inv_unit_lower_quads([a_mat[it] for it in items])
    uw = {}
    for it, ti in zip(items, tinv):
        rhs = jnp.concatenate([_block_diag_rows(piece(vb, *it)),
                               _block_diag_rows(piece(kbg, *it))], axis=1)
        uw[it] = jnp.dot(ti.astype(BF16), rhs, preferred_element_type=F32)
    yield

    chains = [(b, u) for b in range(nb) for u in range(2)]
    states = {ch: state_ref[ch[0], ch[1]] for ch in chains}
    outs = {}
    for ci in range(n_chunks):
        wq = {(b, u): jnp.dot(
            jnp.concatenate([uw[(b, ci, u)][:, QUAD_W:], piece(qd, b, ci, u)], axis=0).astype(BF16),
            _block_diag_rows(states[(b, u)]), preferred_element_type=F32) for b, u in chains}
        yield
        v_new = {(b, u): uw[(b, ci, u)][:, 0:QUAD_W] - wq[(b, u)][0:L] for b, u in chains}
        for b, u in chains:
            gq = piece(g_x, b, ci, u)
            g_last = gq[L - 1:L, :]
            upd = _dot_tn(piece(k, b, ci, u) * jnp.exp(g_last - gq), v_new[(b, u)])
            states[(b, u)] = states[(b, u)] * jnp.exp(g_last) + _fold_block_diag(upd)
        yield
        for b, u in chains:
            outs[(b, ci, u)] = wq[(b, u)][L:2 * L] + jnp.dot(
                attn[(b, ci, u)], _block_diag_rows(v_new[(b, u)]), preferred_element_type=F32)
        yield
    for b, u in chains:
        state_ref[b, u] = states[(b, u)]

    o_all = jnp.concatenate(
        [jnp.concatenate([outs[(b, ci, 0)], outs[(b, ci, 1)]], axis=1)
         for b in range(nb) for ci in range(n_chunks)], axis=0)
    y = o_all * lax.rsqrt(_head_sum(o_all * o_all, ones_bd_ref) * (1.0 / HEAD_W) + NORM_EPS)
    y = y * gnw_ref[...] * _silu(z)
    o_ref[...] = y.reshape(nb, T, BRANCH_W).astype(o_ref.dtype)
    yield


def _ssd_pieces(h, wz_ref, wxbc_ref, wsm_ref, convw_ref, convb_ref, alog_ref, dtb_ref,
                dskip_ref, gnw_ref, tri_ref, o_ref, cbuf_ref, state_ref, nb):
    T = TILE_MIX
    L = SSM_CHUNK
    assert T == L
    R = nb * T
    C = BRANCH_W + 2 * SSM_GROUPS * SSM_STATE

    pre = jnp.dot(h, wxbc_ref[...], preferred_element_type=F32)
    yield
    z = jnp.dot(h, wz_ref[...], preferred_element_type=F32)
    yield
    half = C // 2
    parts = []
    for ci in range(2):
        sl = slice(ci * half, (ci + 1) * half)
        conv = jnp.concatenate(
            [_causal_conv(pre[b * T:(b + 1) * T, sl], cbuf_ref.at[b, :, sl], convw_ref.at[:, sl], T)
             for b in range(nb)], axis=0)
        parts.append(_silu(conv + convb_ref[:, sl]))
        yield
    xbc = jnp.concatenate(parts, axis=1)
    xs = xbc[:, 0:BRANCH_W]
    b_pair = xbc[:, BRANCH_W:BRANCH_W + 128]
    c_pair = xbc[:, BRANCH_W + 128:BRANCH_W + 256]
    small = jnp.dot(h, wsm_ref[...], preferred_element_type=F32)
    dt = _softplus(small + dtb_ref[...])
    a = -jnp.exp(alog_ref[...]) * dt
    acs = _chunk_cumsum(a, tri_ref)
    dt_x = _expand_heads(dt, 0, R)
    acs_x = _expand_heads(acs, 0, R)
    xdt = xs * dt_x
    eacs_x = jnp.exp(acs_x)
    yield

    r = _iota((L, L), 0)
    c = _iota((L, L), 1)
    diag = r == c
    lower = r >= c
    lane_low = _iota((L, 128), 1) < SSM_STATE

    items = [(b, g) for b in range(nb) for g in range(SSM_GROUPS)]

    def rows(t, b):
        return t[b * T:(b + 1) * T]

    def keep(t, g):
        return jnp.where(lane_low if g == 0 else jnp.logical_not(lane_low), t, 0.0)

    c_g = {(b, g): keep(rows(c_pair, b), g) for b, g in items}
    cb = {(b, g): _dot_nt(c_g[(b, g)], rows(b_pair, b)) for b, g in items}
    y_off = {(b, g): _dot(c_g[(b, g)], state_ref[b, g]) for b, g in items}
    yield
    ys = {}
    for b in range(nb):
        for g in range(SSM_GROUPS):
            parts = []
            for e in range(4):
                hd = 4 * g + e
                col = jnp.broadcast_to(rows(acs, b)[:, hd:hd + 1], (L, L))
                row = jnp.sum(jnp.where(diag, col, 0.0), axis=0, keepdims=True)
                parts.append(cb[(b, g)] * jnp.exp(jnp.where(lower, col - row, -jnp.inf)))
            m_quad = jnp.concatenate(parts, axis=1).astype(BF16)
            sl = slice(g * QUAD_W, (g + 1) * QUAD_W)
            xq = rows(xdt, b)[:, sl]
            acs_q = rows(acs_x, b)[:, sl]
            last_q = acs_q[L - 1:L, :]
            y_diag = jnp.dot(m_quad, _block_diag_rows(xq), preferred_element_type=F32)
            states = _dot_tn(keep(rows(b_pair, b), g), xq * jnp.exp(last_q - acs_q))
            state_ref[b, g] = state_ref[b, g] * jnp.exp(last_q) + states
            ys[(b, g)] = y_diag + y_off[(b, g)] * rows(eacs_x, b)[:, sl]
        yield
    y = jnp.concatenate([jnp.concatenate([ys[(b, 0)], ys[(b, 1)]], axis=1) for b in range(nb)], axis=0)
    y = (y + dskip_ref[...] * xs) * _silu(z)
    outs = []
    for g in range(SSM_GROUPS):
        yg = y[:, g * QUAD_W:(g + 1) * QUAD_W]
        outs.append(yg * lax.rsqrt(jnp.mean(yg * yg, axis=-1, keepdims=True) + NORM_EPS))
    y = jnp.concatenate(outs, axis=1) * gnw_ref[...]
    o_ref[...] = y.reshape(nb, T, BRANCH_W).astype(o_ref.dtype)
    yield


def _rotate_pairs(t, cos, sin_signed):
    n = t.shape[1]
    lo = (_iota(t.shape, 1) % HEAD_W) < (HEAD_W // 2)
    partner = jnp.where(lo, pltpu.roll(t, n - HEAD_W // 2, 1), pltpu.roll(t, HEAD_W // 2, 1))
    return t * cos + partner * sin_signed


def _ret_pieces(h, w_ref, cos_ref, sin_ref, dmat_ref, qdec_ref, kdec_ref, cgam_ref,
                ones_bd_ref, o_ref, state_ref, nb):
    T = TILE_MIX
    L = RET_CHUNK
    assert T == L
    W = BRANCH_W

    proj = []
    for gi in range(4):
        proj.append(jnp.dot(h, w_ref[:, gi * W:(gi + 1) * W], preferred_element_type=F32))
        yield
    cos = jnp.concatenate([cos_ref[...]] * nb, axis=0)
    sin = jnp.concatenate([sin_ref[...]] * nb, axis=0)
    q = _rotate_pairs(proj[0], cos, sin)
    yield
    k = _rotate_pairs(proj[1], cos, sin) * (HEAD_W ** -0.5)
    yield
    v = proj[2]
    gate = proj[3]

    items = [(b, u) for b in range(nb) for u in range(2)]

    def piece(t, b, u):
        return t[b * T:(b + 1) * T, u * QUAD_W:(u + 1) * QUAD_W]

    ys = {}
    for b0 in range(0, nb, 2):
        its = [it for it in items if b0 <= it[0] < b0 + 2]
        sc = {it: _dot_nt(piece(q, *it), _block_diag_rows(piece(k, *it))) for it in its}
        y_off = {}
        for b, u in its:
            sl = slice(u * QUAD_W, (u + 1) * QUAD_W)
            y_off[(b, u)] = jnp.dot((piece(q, b, u) * qdec_ref[:, sl]).astype(BF16),
                                    _block_diag_rows(state_ref[b, u]), preferred_element_type=F32)
        yield
        for b, u in its:
            sl = slice(u * QUAD_W, (u + 1) * QUAD_W)
            a = (sc[(b, u)] * dmat_ref[u]).astype(BF16)
            ys[(b, u)] = y_off[(b, u)] + jnp.dot(a, _block_diag_rows(piece(v, b, u)),
                                                 preferred_element_type=F32)
            kv = _dot_tn(piece(k, b, u) * kdec_ref[:, sl], piece(v, b, u))
            state_ref[b, u] = state_ref[b, u] * cgam_ref[:, sl] + _fold_block_diag(kv)
        yield
    y = jnp.concatenate([jnp.concatenate([ys[(b, 0)], ys[(b, 1)]], axis=1) for b in range(nb)], axis=0)
    y = y * lax.rsqrt(_head_sum(y * y, ones_bd_ref) * (1.0 / HEAD_W) + NORM_EPS) * _silu(gate)
    o_ref[...] = y.reshape(nb, T, BRANCH_W).astype(o_ref.dtype)
    yield


_PIECE_ORDER = ("AAAAAAA" "CACACACA" "BABA" "ABACABACABABABABABABAB" "ACACBACACBACACBCC" "ABC")


def _mixers_kernel(x_ref, nw_ref,
                   a_wqkv, a_wz, a_wsm, a_convw, a_alog, a_dtb, a_gnw, a_tri,
                   b_wz, b_wxbc, b_wsm, b_convw, b_convb, b_alog, b_dtb, b_dskip, b_gnw, b_tri,
                   c_w, c_cos, c_sin, c_dmat, c_qdec, c_kdec, c_cgam, ones_bd_ref,
                   oa_ref, ob_ref, oc_ref,
                   a_cbuf, a_state, b_cbuf, b_state, c_state, *, nb):
    R = nb * TILE_MIX

    @pl.when(pl.program_id(1) == 0)
    def _():
        for ref in (a_state, b_state, c_state, a_cbuf, b_cbuf):
            ref[...] = jnp.zeros_like(ref)

    h = _rmsnorm_rows(x_ref[...].reshape(R, D_MODEL), nw_ref[...]).astype(BF16)
    gens = {
        "A": _gdn_pieces(h, a_wqkv, a_wz, a_wsm, a_convw, a_alog, a_dtb, a_gnw, ones_bd_ref,
                         a_tri, oa_ref, a_cbuf, a_state, nb),
        "B": _ssd_pieces(h, b_wz, b_wxbc, b_wsm, b_convw, b_convb, b_alog, b_dtb, b_dskip,
                         b_gnw, b_tri, ob_ref, b_cbuf, b_state, nb),
        "C": _ret_pieces(h, c_w, c_cos, c_sin, c_dmat, c_qdec, c_kdec, c_cgam, ones_bd_ref,
                         oc_ref, c_state, nb),
    }
    for tag in _PIECE_ORDER:
        if tag in gens and next(gens[tag], "done") == "done":
            del gens[tag]
    for gen in gens.values():
        for _ in gen:
            pass


def _post_kernel(x_ref, ya_ref, yb_ref, yc_ref, p_ref, nmix_ref, wgate_ref, wbr_ref, wout_ref,
                 nmlp_ref, wup_ref, wdown_ref, nple_ref, wpg_ref, wpp_ref, nfin_ref, o_ref,
                 *, final_norm):
    x = x_ref[...]
    h = _rmsnorm_rows(x, nmix_ref[...]).astype(BF16)
    mixed = None
    for n, y_ref in enumerate((ya_ref, yb_ref, yc_ref)):
        logits = jnp.dot(h, wgate_ref[:, n * D_MODEL:(n + 1) * D_MODEL], preferred_element_type=F32)
        branch = jnp.dot(y_ref[...], wbr_ref[n], preferred_element_type=F32)
        term = _sigmoid(logits) * branch
        mixed = term if mixed is None else mixed + term
    x = x + jnp.dot(mixed.astype(BF16), wout_ref[...], preferred_element_type=F32)

    h = _rmsnorm_rows(x, nmlp_ref[...]).astype(BF16)
    acc = None
    for f in range(D_FF // FF_BLOCK):
        u = jnp.maximum(jnp.dot(h, wup_ref[:, f * FF_BLOCK:(f + 1) * FF_BLOCK],
                                preferred_element_type=F32), 0.0)
        d = jnp.dot((u * u).astype(BF16), wdown_ref[f * FF_BLOCK:(f + 1) * FF_BLOCK, :],
                    preferred_element_type=F32)
        acc = d if acc is None else acc + d
    x = x + acc

    h = _rmsnorm_rows(x, nple_ref[...]).astype(BF16)
    ple = jnp.dot(p_ref[...].astype(BF16), wpp_ref[...], preferred_element_type=F32)
    x = x + _sigmoid(jnp.dot(h, wpg_ref[...], preferred_element_type=F32)) * ple
    if final_norm:
        x = _rmsnorm_rows(x, nfin_ref[...])
    o_ref[...] = x


def _const_spec(shape):
    zeros = (0,) * len(shape)
    return pl.BlockSpec(shape, lambda *_: zeros, pipeline_mode=pl.Buffered(1))


def _ones_block_diag():
    i = np.arange(BRANCH_W) // HEAD_W
    return jnp.asarray((i[:, None] == i[None, :]).astype(np.float32), dtype=BF16)


def _chunk_tri(rows, chunk):
    i = np.arange(rows)
    m = (i[:, None] >= i[None, :]) & ((i[:, None] // chunk) == (i[None, :] // chunk))
    return jnp.asarray(m.astype(np.float32), dtype=BF16)


def _lane_vec(v, first_lane):
    out = jnp.zeros((1, SMALL_W), F32)
    return out.at[0, first_lane:first_lane + v.shape[0]].set(v.astype(F32))


def _pad_cols(w, first_lane):
    out = jnp.zeros((w.shape[0], SMALL_W), w.dtype)
    return out.at[:, first_lane:first_lane + w.shape[1]].set(w)


def _retention_tables(S):
    half = HEAD_W // 2
    inv_freq = ROPE_BASE ** (-jnp.linspace(0.0, 1.0, half, dtype=F32))
    ang = jnp.arange(S, dtype=F32)[:, None] * inv_freq[None, :]
    cos = jnp.tile(jnp.cos(ang), (1, 2 * N_HEADS))
    sin = jnp.sin(ang)
    sin_signed = jnp.tile(jnp.concatenate([-sin, sin], axis=1), (1, N_HEADS))
    L = RET_CHUNK
    log_gamma = jnp.log1p(-jnp.exp2(-5.0 - jnp.arange(N_HEADS, dtype=F32)))
    idx = jnp.arange(L, dtype=F32)
    tril = jnp.tril(jnp.ones((L, L), bool))
    dmat = jnp.exp(jnp.where(tril, (idx[:, None] - idx[None, :]) * log_gamma[:, None, None], -jnp.inf))
    dmat = dmat.reshape(2, 4, L, L).transpose(0, 2, 1, 3).reshape(2, L, 4 * L)
    kdec = jnp.repeat(jnp.exp((L - 1 - idx)[:, None] * log_gamma[None, :]), HEAD_W, axis=1)
    qdec = jnp.repeat(jnp.exp((idx + 1)[:, None] * log_gamma[None, :]), HEAD_W, axis=1)
    cgam = jnp.repeat(jnp.exp(L * log_gamma)[None, :], HEAD_W, axis=1)
    return cos, sin_signed, dmat, qdec, kdec, cgam


def kernel(x, p, norm_mix, w_in, gdn_conv_w, gdn_A_log, gdn_dt_bias, gdn_norm_w, ssm_conv_w,
           ssm_conv_b, ssm_A_log, ssm_dt_bias, ssm_D, ssm_norm_w, w_branch, w_out, norm_mlp,
           w_up, w_down, norm_ple, w_ple_gate, w_ple_proj, norm_final):
    B, S, D = x.shape
    depth = w_in.shape[0]
    assert D == D_MODEL and S % TILE_MIX == 0 and B % min(B, MIX_ROWS) == 0 and (B * S) % TILE_POST == 0
    ones_bd = _ones_block_diag()
    tri_gdn = _chunk_tri(128, GDN_CHUNK)
    tri_ssd = _chunk_tri(128, SSM_CHUNK)
    cos, sin_signed, dmat, qdec, kdec, cgam = _retention_tables(S)
    row = lambda v: v.reshape(1, -1).astype(F32)

    sizes = (3 * BRANCH_W, BRANCH_W, N_HEADS, N_HEADS, BRANCH_W, BRANCH_W + 2 * SSM_GROUPS * SSM_STATE,
             N_HEADS, BRANCH_W, BRANCH_W, BRANCH_W, BRANCH_W, 3 * D_MODEL)
    offs = np.concatenate([[0], np.cumsum(sizes)])
    seg = lambda w, i: w[:, offs[i]:offs[i + 1]]

    for i in range(depth):
        w = w_in[i].astype(BF16)
        nmix = row(norm_mix[i])
        consts_a = (seg(w, 0), seg(w, 1), _pad_cols(jnp.concatenate([seg(w, 2), seg(w, 3)], axis=1), 0),
                    gdn_conv_w[i].astype(F32), _lane_vec(gdn_A_log[i], N_HEADS),
                    _lane_vec(gdn_dt_bias[i], N_HEADS), row(jnp.tile(gdn_norm_w[i], N_HEADS)), tri_gdn)
        consts_b = (seg(w, 4), seg(w, 5), _pad_cols(seg(w, 6), 0), ssm_conv_w[i].astype(F32),
                    row(ssm_conv_b[i]), _lane_vec(ssm_A_log[i], 0), _lane_vec(ssm_dt_bias[i], 0),
                    row(jnp.repeat(ssm_D[i], HEAD_W)), row(ssm_norm_w[i]), tri_ssd)
        consts_c = (dmat, qdec, kdec, cgam, ones_bd)
        w_c = w[:, offs[7]:offs[11]]
        nb = min(B, MIX_ROWS)
        pos_spec = pl.BlockSpec((TILE_MIX, BRANCH_W), lambda g, j: (j, 0))
        y_spec = pl.BlockSpec((nb, TILE_MIX, BRANCH_W), lambda g, j: (g, j, 0))
        y_shape = jax.ShapeDtypeStruct((B, S, BRANCH_W), BF16)
        xbc_w = BRANCH_W + 2 * SSM_GROUPS * SSM_STATE
        y_a, y_b, y_c = pl.pallas_call(
            functools.partial(_mixers_kernel, nb=nb),
            out_shape=(y_shape, y_shape, y_shape),
            grid=(B // nb, S // TILE_MIX),
            in_specs=[pl.BlockSpec((nb, TILE_MIX, D), lambda g, j: (g, j, 0)), _const_spec(nmix.shape)]
            + [_const_spec(c.shape) for c in consts_a + consts_b] + [_const_spec(w_c.shape), pos_spec, pos_spec]
            + [_const_spec(c.shape) for c in consts_c],
            out_specs=(y_spec, y_spec, y_spec),
            scratch_shapes=[pltpu.VMEM((nb, CONV_PAD, 3 * BRANCH_W), F32),
                            pltpu.VMEM((nb, 2, HEAD_W, QUAD_W), F32),
                            pltpu.VMEM((nb, CONV_PAD, xbc_w), F32),
                            pltpu.VMEM((nb, SSM_GROUPS, 128, QUAD_W), F32),
                            pltpu.VMEM((nb, 2, HEAD_W, QUAD_W), F32)],
            compiler_params=pltpu.CompilerParams(
                dimension_semantics=("arbitrary", "arbitrary"), vmem_limit_bytes=VMEM_LIMIT_MIXER),
            name="mixers",
        )(x, nmix, *consts_a, *consts_b, w_c, cos, sin_signed, *consts_c)
        n_tok = B * S
        tok = lambda a: a.reshape(n_tok, a.shape[-1])
        consts = (nmix, seg(w, 11), w_branch[i].astype(BF16), w_out[i].astype(BF16), row(norm_mlp[i]),
                  w_up[i].astype(BF16), w_down[i].astype(BF16), row(norm_ple[i]),
                  w_ple_gate[i].astype(BF16), w_ple_proj[i].astype(BF16), row(norm_final))
        tile_spec = lambda width: pl.BlockSpec((TILE_POST, width), lambda t: (t, 0))
        x = pl.pallas_call(
            functools.partial(_post_kernel, final_norm=(i == depth - 1)),
            out_shape=jax.ShapeDtypeStruct((n_tok, D), F32),
            grid=(n_tok // TILE_POST,),
            in_specs=[tile_spec(D), tile_spec(BRANCH_W), tile_spec(BRANCH_W), tile_spec(BRANCH_W),
                      tile_spec(PLE_DIM)] + [_const_spec(c.shape) for c in consts],
            out_specs=tile_spec(D),
            compiler_params=pltpu.CompilerParams(
                dimension_semantics=("arbitrary",), vmem_limit_bytes=VMEM_LIMIT_POST),
            name="merge_mlp_ple",
        )(tok(x), tok(y_a), tok(y_b), tok(y_c), tok(p[i]), *consts).reshape(B, S, D)
    return x
```

```python
import functools
import math

import numpy as np
import jax
import jax.numpy as jnp
from jax import lax
from jax.experimental import pallas as pl
from jax.experimental.pallas import tpu as pltpu

F32 = jnp.float32
BF16 = jnp.bfloat16

D_MODEL = 1024
PLE_DIM = 256
D_FF = 4 * D_MODEL
NORM_EPS = 1e-6
N_HEADS = 8
HEAD_W = 64
BRANCH_W = N_HEADS * HEAD_W
QUAD_W = 4 * HEAD_W
SMALL_W = 128
GDN_CHUNK = 64
SSM_CHUNK = 128
RET_CHUNK = 128
SSM_GROUPS = 2
SSM_STATE = 64
ROPE_BASE = 10000.0
CONV_K = 4
CONV_PAD = 8

TILE_MIX = 128
MIX_ROWS = 4
TILE_POST = 512
FF_BLOCK = 1024
VMEM_LIMIT_MIXER = 56 * 1024 * 1024
VMEM_LIMIT_POST = 56 * 1024 * 1024


def _rmsnorm_rows(x, w):
    y = x * lax.rsqrt(jnp.mean(x * x, axis=-1, keepdims=True) + NORM_EPS)
    return y * w


def _sigmoid(x):
    return 1.0 / (1.0 + jnp.exp(-x))


def _silu(x):
    hx = 0.5 * x
    return hx + hx * jnp.tanh(hx)


def _softplus(x):
    return jnp.maximum(x, 0.0) + jnp.log1p(jnp.exp(-jnp.abs(x)))


def _dot(a, b):
    return jnp.dot(a.astype(BF16), b.astype(BF16), preferred_element_type=F32)


def _dot_nt(a, b):
    return lax.dot_general(a.astype(BF16), b.astype(BF16), (((1,), (1,)), ((), ())),
                           preferred_element_type=F32)


def _dot_tn(a, b):
    return lax.dot_general(a.astype(BF16), b.astype(BF16), (((0,), (0,)), ((), ())),
                           preferred_element_type=F32)


def _chunk_cumsum(x, tri_ref):
    x1 = x.astype(BF16)
    x2 = (x - x1.astype(F32)).astype(BF16)
    tri = tri_ref[...]
    outs = []
    for s in range(x.shape[0] // 128):
        sl = slice(s * 128, (s + 1) * 128)
        outs.append(jnp.dot(tri, x1[sl], preferred_element_type=F32)
                    + jnp.dot(tri, x2[sl], preferred_element_type=F32))
    return jnp.concatenate(outs, axis=0)


def _iota(shape, dim):
    return lax.broadcasted_iota(jnp.int32, shape, dim)


def _block_diag_rows(y):
    head = _iota(y.shape, 1) // HEAD_W
    yb = y.astype(BF16)
    zero = jnp.zeros_like(yb)
    return jnp.concatenate([jnp.where(head == e, yb, zero) for e in range(4)], axis=0)


def _expand_heads(cols, first_lane, rows):
    low = _iota((rows, 128), 1) < HEAD_W
    parts = []
    for p in range(N_HEADS // 2):
        a = jnp.broadcast_to(cols[:, first_lane + 2 * p:first_lane + 2 * p + 1], (rows, 128))
        b = jnp.broadcast_to(cols[:, first_lane + 2 * p + 1:first_lane + 2 * p + 2], (rows, 128))
        parts.append(jnp.where(low, a, b))
    return jnp.concatenate(parts, axis=1)


def _causal_conv(pre, carry_ref, w_ref, rows):
    cat = jnp.concatenate([carry_ref[...], pre], axis=0)
    acc = pre * w_ref[CONV_K - 1:CONV_K, :]
    for s in range(1, CONV_K):
        shifted = pltpu.roll(cat, s, 0)[CONV_PAD:CONV_PAD + rows, :]
        acc = acc + shifted * w_ref[CONV_K - 1 - s:CONV_K - s, :]
    carry_ref[...] = pre[rows - CONV_PAD:rows, :]
    return acc


def _fold_block_diag(m):
    head = _iota((HEAD_W, QUAD_W), 1) // HEAD_W
    out = jnp.where(head == 0, m[0:HEAD_W], 0.0)
    for e in range(1, 4):
        out = out + jnp.where(head == e, m[e * HEAD_W:(e + 1) * HEAD_W], 0.0)
    return out


def _head_sum(sq, ones_bd_ref):
    ones = ones_bd_ref[...]
    return jnp.concatenate(
        [jnp.dot(sq[:, i:i + QUAD_W].astype(BF16), ones, preferred_element_type=F32)
         for i in range(0, sq.shape[1], QUAD_W)], axis=1)


def _inv_unit_lower_quads(mats):
    shape = mats[0].shape
    r = _iota(shape, 0)
    c = _iota(shape, 1) % HEAD_W
    eye = jnp.where(r == c, 1.0, 0.0).astype(F32)
    in16 = (r // 16) == (c // 16)
    in32 = (r // 32) == (c // 32)
    off16 = jnp.logical_and(in32, jnp.logical_not(in16))

    def mm(xs, ys):
        return [jnp.dot(x.astype(BF16), _block_diag_rows(y), preferred_element_type=F32)
                for x, y in zip(xs, ys)]

    def mm2(xs, ps):
        outs = mm([jnp.concatenate([x, p], axis=0) for x, p in zip(xs, ps)], ps)
        n = shape[0]
        return [o[0:n] for o in outs], [o[n:2 * n] for o in outs]

    ns = [jnp.where(in16, -a, 0.0) for a in mats]
    xs = [eye + n for n in ns]
    ps = mm(ns, ns)
    yield
    for _ in range(2):
        xp, ps = mm2(xs, ps)
        xs = [x + d for x, d in zip(xs, xp)]
        yield
    xs = [x + d for x, d in zip(xs, mm(xs, ps))]
    yield
    for keep in (off16, jnp.logical_not(in32)):
        ls = [jnp.where(keep, a, 0.0) for a in mats]
        ts = mm(xs, ls)
        yield
        xs = [x - d for x, d in zip(xs, mm(ts, xs))]
        yield
    return xs


def _gdn_pieces(h, wqkv_ref, wz_ref, wsm_ref, convw_ref, alog_ref, dtb_ref, gnw_ref,
                ones_bd_ref, tri_ref, o_ref, cbuf_ref, state_ref, nb):
    T = TILE_MIX
    L = GDN_CHUNK
    n_chunks = T // L
    R = nb * T
    W = BRANCH_W

    def project(gi):
        return jnp.dot(h, wqkv_ref[:, gi * W:(gi + 1) * W], preferred_element_type=F32)

    def conv_silu(pre, gi):
        conv = jnp.concatenate(
            [_causal_conv(pre[b * T:(b + 1) * T], cbuf_ref.at[b, :, gi * W:(gi + 1) * W],
                          convw_ref.at[:, gi * W:(gi + 1) * W], T) for b in range(nb)], axis=0)
        return _silu(conv)

    pre_q = project(0)
    yield
    pre_k = project(1)
    yield
    q = conv_silu(pre_q, 0)
    yield
    pre_v = project(2)
    yield
    k = conv_silu(pre_k, 1)
    yield
    z = jnp.dot(h, wz_ref[...], preferred_element_type=F32)
    yield
    v = conv_silu(pre_v, 2)
    yield
    q = q * lax.rsqrt(_head_sum(q * q, ones_bd_ref) + 1e-6) * (HEAD_W ** -0.5)
    yield
    k = k * lax.rsqrt(_head_sum(k * k, ones_bd_ref) + 1e-6)
    yield

    small = jnp.dot(h, wsm_ref[...], preferred_element_type=F32)
    beta = _sigmoid(small)
    glog = -jnp.exp(alog_ref[...]) * _softplus(small + dtb_ref[...])
    gcum = _chunk_cumsum(glog, tri_ref)
    beta_x = _expand_heads(beta, 0, R)
    g_x = _expand_heads(gcum, N_HEADS, R)
    yield
    kb = k * beta_x
    vb = v * beta_x
    eg = jnp.exp(g_x)
    kbg = kb * eg
    qd = q * eg
    yield

    shape = (L, QUAD_W)
    r = _iota(shape, 0)
    c = _iota(shape, 1) % HEAD_W
    diag = r == c
    lower = r >= c
    strict = r > c

    def piece(t, b, ci, u):
        r0 = b * T + ci * L
        return t[r0:r0 + L, u * QUAD_W:(u + 1) * QUAD_W]

    items = [(b, ci, u) for ci in range(n_chunks) for b in range(nb) for u in range(2)]
    a_mat, attn = {}, {}
    for ci in range(n_chunks):
        its = [it for it in items if it[1] == ci]
        scs = [_dot_nt(jnp.concatenate([piece(kb, *it), piece(q, *it)], axis=0),
                       _block_diag_rows(piece(k, *it))) for it in its]
        for it, sc in zip(its, scs):
            gq = piece(g_x, *it)
            g_row = jnp.sum(jnp.where(diag, gq, 0.0), axis=0, keepdims=True)
            dec = jnp.exp(jnp.where(lower, gq - g_row, -jnp.inf))
            a_mat[it] = jnp.where(strict, sc[0:L] * dec, 0.0)
            attn[it] = (sc[L:2 * L] * dec).astype(BF16)
        yield
    tinv = yield from _inv_unit_lower_quads([a_mat[it] for it in items])
    uw = {}
    for it, ti in zip(items, tinv):
        rhs = jnp.concatenate([_block_diag_rows(piece(vb, *it)),
                               _block_diag_rows(piece(kbg, *it))], axis=1)
        uw[it] = jnp.dot(ti.astype(BF16), rhs, preferred_element_type=F32)
    yield

    chains = [(b, u) for b in range(nb) for u in range(2)]
    states = {ch: state_ref[ch[0], ch[1]] for ch in chains}
    outs = {}
    for ci in range(n_chunks):
        wq = {(b, u): jnp.dot(
            jnp.concatenate([uw[(b, ci, u)][:, QUAD_W:], piece(qd, b, ci, u)], axis=0).astype(BF16),
            _block_diag_rows(states[(b, u)]), preferred_element_type=F32) for b, u in chains}
        yield
        v_new = {(b, u): uw[(b, ci, u)][:, 0:QUAD_W] - wq[(b, u)][0:L] for b, u in chains}
        for b, u in chains:
            gq = piece(g_x, b, ci, u)
            g_last = gq[L - 1:L, :]
            upd = _dot_tn(piece(k, b, ci, u) * jnp.exp(g_last - gq), v_new[(b, u)])
            states[(b, u)] = states[(b, u)] * jnp.exp(g_last) + _fold_block_diag(upd)
        yield
        for b, u in chains:
            outs[(b, ci, u)] = wq[(b, u)][L:2 * L] + jnp.dot(
                attn[(b, ci, u)], _block_diag_rows(v_new[(b, u)]), preferred_element_type=F32)
        yield
    for b, u in chains:
        state_ref[b, u] = states[(b, u)]

    o_all = jnp.concatenate(
        [jnp.concatenate([outs[(b, ci, 0)], outs[(b, ci, 1)]], axis=1)
         for b in range(nb) for ci in range(n_chunks)], axis=0)
    y = o_all * lax.rsqrt(_head_sum(o_all * o_all, ones_bd_ref) * (1.0 / HEAD_W) + NORM_EPS)
    y = y * gnw_ref[...] * _silu(z)
    o_ref[...] = y.reshape(nb, T, BRANCH_W).astype(o_ref.dtype)
    yield


def _ssd_pieces(h, wz_ref, wxbc_ref, wsm_ref, convw_ref, convb_ref, alog_ref, dtb_ref,
                dskip_ref, gnw_ref, tri_ref, o_ref, cbuf_ref, state_ref, nb):
    T = TILE_MIX
    L = SSM_CHUNK
    assert T == L
    R = nb * T
    C = BRANCH_W + 2 * SSM_GROUPS * SSM_STATE

    pre = jnp.dot(h, wxbc_ref[...], preferred_element_type=F32)
    yield
    z = jnp.dot(h, wz_ref[...], preferred_element_type=F32)
    yield
    half = C // 2
    parts = []
    for ci in range(2):
        sl = slice(ci * half, (ci + 1) * half)
        conv = jnp.concatenate(
            [_causal_conv(pre[b * T:(b + 1) * T, sl], cbuf_ref.at[b, :, sl], convw_ref.at[:, sl], T)
             for b in range(nb)], axis=0)
        parts.append(_silu(conv + convb_ref[:, sl]))
        yield
    xbc = jnp.concatenate(parts, axis=1)
    xs = xbc[:, 0:BRANCH_W]
    b_pair = xbc[:, BRANCH_W:BRANCH_W + 128]
    c_pair = xbc[:, BRANCH_W + 128:BRANCH_W + 256]
    small = jnp.dot(h, wsm_ref[...], preferred_element_type=F32)
    dt = _softplus(small + dtb_ref[...])
    a = -jnp.exp(alog_ref[...]) * dt
    acs = _chunk_cumsum(a, tri_ref)
    dt_x = _expand_heads(dt, 0, R)
    acs_x = _expand_heads(acs, 0, R)
    xdt = xs * dt_x
    eacs_x = jnp.exp(acs_x)
    yield

    r = _iota((L, L), 0)
    c = _iota((L, L), 1)
    diag = r == c
    lower = r >= c
    lane_low = _iota((L, 128), 1) < SSM_STATE

    items = [(b, g) for b in range(nb) for g in range(SSM_GROUPS)]

    def rows(t, b):
        return t[b * T:(b + 1) * T]

    def keep(t, g):
        return jnp.where(lane_low if g == 0 else jnp.logical_not(lane_low), t, 0.0)

    c_g = {(b, g): keep(rows(c_pair, b), g) for b, g in items}
    cb = {(b, g): _dot_nt(c_g[(b, g)], rows(b_pair, b)) for b, g in items}
    y_off = {(b, g): _dot(c_g[(b, g)], state_ref[b, g]) for b, g in items}
    yield
    ys = {}
    for b in range(nb):
        for g in range(SSM_GROUPS):
            parts = []
            for e in range(4):
                hd = 4 * g + e
                col = jnp.broadcast_to(rows(acs, b)[:, hd:hd + 1], (L, L))
                row = jnp.sum(jnp.where(diag, col, 0.0), axis=0, keepdims=True)
                parts.append(cb[(b, g)] * jnp.exp(jnp.where(lower, col - row, -jnp.inf)))
            m_quad = jnp.concatenate(parts, axis=1).astype(BF16)
            sl = slice(g * QUAD_W, (g + 1) * QUAD_W)
            xq = rows(xdt, b)[:, sl]
            acs_q = rows(acs_x, b)[:, sl]
            last_q = acs_q[L - 1:L, :]
            y_diag = jnp.dot(m_quad, _block_diag_rows(xq), preferred_element_type=F32)
            states = _dot_tn(keep(rows(b_pair, b), g), xq * jnp.exp(last_q - acs_q))
            state_ref[b, g] = state_ref[b, g] * jnp.exp(last_q) + states
            ys[(b, g)] = y_diag + y_off[(b, g)] * rows(eacs_x, b)[:, sl]
        yield
    y = jnp.concatenate([jnp.concatenate([ys[(b, 0)], ys[(b, 1)]], axis=1) for b in range(nb)], axis=0)
    y = (y + dskip_ref[...] * xs) * _silu(z)
    outs = []
    for g in range(SSM_GROUPS):
        yg = y[:, g * QUAD_W:(g + 1) * QUAD_W]
        outs.append(yg * lax.rsqrt(jnp.mean(yg * yg, axis=-1, keepdims=True) + NORM_EPS))
    y = jnp.concatenate(outs, axis=1) * gnw_ref[...]
    o_ref[...] = y.reshape(nb, T, BRANCH_W).astype(o_ref.dtype)
    yield


def _rotate_pairs(t, cos, sin_signed):
    n = t.shape[1]
    lo = (_iota(t.shape, 1) % HEAD_W) < (HEAD_W // 2)
    partner = jnp.where(lo, pltpu.roll(t, n - HEAD_W // 2, 1), pltpu.roll(t, HEAD_W // 2, 1))
    return t * cos + partner * sin_signed


def _ret_pieces(h, w_ref, cos_ref, sin_ref, dmat_ref, qdec_ref, kdec_ref, cgam_ref,
                ones_bd_ref, o_ref, state_ref, nb):
    T = TILE_MIX
    L = RET_CHUNK
    assert T == L
    W = BRANCH_W

    proj = []
    for gi in range(4):
        proj.append(jnp.dot(h, w_ref[:, gi * W:(gi + 1) * W], preferred_element_type=F32))
        yield
    cos = jnp.concatenate([cos_ref[...]] * nb, axis=0)
    sin = jnp.concatenate([sin_ref[...]] * nb, axis=0)
    q = _rotate_pairs(proj[0], cos, sin)
    yield
    k = _rotate_pairs(proj[1], cos, sin) * (HEAD_W ** -0.5)
    yield
    v = proj[2]
    gate = proj[3]

    items = [(b, u) for b in range(nb) for u in range(2)]

    def piece(t, b, u):
        return t[b * T:(b + 1) * T, u * QUAD_W:(u + 1) * QUAD_W]

    ys = {}
    for b0 in range(0, nb, 2):
        its = [it for it in items if b0 <= it[0] < b0 + 2]
        sc = {it: _dot_nt(piece(q, *it), _block_diag_rows(piece(k, *it))) for it in its}
        y_off = {}
        for b, u in its:
            sl = slice(u * QUAD_W, (u + 1) * QUAD_W)
            y_off[(b, u)] = jnp.dot((piece(q, b, u) * qdec_ref[:, sl]).astype(BF16),
                                    _block_diag_rows(state_ref[b, u]), preferred_element_type=F32)
        yield
        for b, u in its:
            sl = slice(u * QUAD_W, (u + 1) * QUAD_W)
            a = (sc[(b, u)] * dmat_ref[u]).astype(BF16)
            ys[(b, u)] = y_off[(b, u)] + jnp.dot(a, _block_diag_rows(piece(v, b, u)),
                                                 preferred_element_type=F32)
            kv = _dot_tn(piece(k, b, u) * kdec_ref[:, sl], piece(v, b, u))
            state_ref[b, u] = state_ref[b, u] * cgam_ref[:, sl] + _fold_block_diag(kv)
        yield
    y = jnp.concatenate([jnp.concatenate([ys[(b, 0)], ys[(b, 1)]], axis=1) for b in range(nb)], axis=0)
    y = y * lax.rsqrt(_head_sum(y * y, ones_bd_ref) * (1.0 / HEAD_W) + NORM_EPS) * _silu(gate)
    o_ref[...] = y.reshape(nb, T, BRANCH_W).astype(o_ref.dtype)
    yield


_MIX_COLS = tuple(np.cumsum([0, 3 * BRANCH_W, BRANCH_W, SMALL_W, BRANCH_W,
                             BRANCH_W + 2 * SSM_GROUPS * SSM_STATE, SMALL_W, 4 * BRANCH_W]).tolist())

_PIECE_ORDER = ("AAAAAAA" "CACACACA" "BABA" "ABACABACABABABABBABB" "ACACBACACBACACBCC" "ABC")


def _mixers_kernel(x_ref, nw_ref, w_ref,
                   a_convw, a_alog, a_dtb, a_gnw, a_tri,
                   b_convw, b_convb, b_alog, b_dtb, b_dskip, b_gnw, b_tri,
                   c_cos, c_sin, c_dmat, c_qdec, c_kdec, c_cgam, ones_bd_ref,
                   oa_ref, ob_ref, oc_ref,
                   a_cbuf, a_state, b_cbuf, b_state, c_state, *, nb):
    R = nb * TILE_MIX

    @pl.when(pl.program_id(1) == 0)
    def _():
        for ref in (a_state, b_state, c_state, a_cbuf, b_cbuf):
            ref[...] = jnp.zeros_like(ref)

    h = _rmsnorm_rows(x_ref[...].reshape(R, D_MODEL), nw_ref[...]).astype(BF16)
    a_wqkv, a_wz, a_wsm, b_wz, b_wxbc, b_wsm, c_w = (
        w_ref.at[:, lo:hi] for lo, hi in zip(_MIX_COLS[:-1], _MIX_COLS[1:]))
    gens = {
        "A": _gdn_pieces(h, a_wqkv, a_wz, a_wsm, a_convw, a_alog, a_dtb, a_gnw, ones_bd_ref,
                         a_tri, oa_ref, a_cbuf, a_state, nb),
        "B": _ssd_pieces(h, b_wz, b_wxbc, b_wsm, b_convw, b_convb, b_alog, b_dtb, b_dskip,
                         b_gnw, b_tri, ob_ref, b_cbuf, b_state, nb),
        "C": _ret_pieces(h, c_w, c_cos, c_sin, c_dmat, c_qdec, c_kdec, c_cgam, ones_bd_ref,
                         oc_ref, c_state, nb),
    }
    for tag in _PIECE_ORDER:
        if tag in gens and next(gens[tag], "done") == "done":
            del gens[tag]
    for gen in gens.values():
        for _ in gen:
            pass


def _post_kernel(x_ref, ya_ref, yb_ref, yc_ref, p_ref, nmix_ref, wgate_ref, wbr_ref, wout_ref,
                 nmlp_ref, wup_ref, wdown_ref, nple_ref, wpg_ref, wpp_ref, nfin_ref, o_ref,
                 *, final_norm):
    x = x_ref[...]
    h = _rmsnorm_rows(x, nmix_ref[...]).astype(BF16)
    mixed = None
    for n, y_ref in enumerate((ya_ref, yb_ref, yc_ref)):
        logits = jnp.dot(h, wgate_ref[:, n * D_MODEL:(n + 1) * D_MODEL], preferred_element_type=F32)
        branch = jnp.dot(y_ref[...], wbr_ref[n], preferred_element_type=F32)
        term = _sigmoid(logits) * branch
        mixed = term if mixed is None else mixed + term
    x = x + jnp.dot(mixed.astype(BF16), wout_ref[...], preferred_element_type=F32)

    h = _rmsnorm_rows(x, nmlp_ref[...]).astype(BF16)
    acc = None
    for f in range(D_FF // FF_BLOCK):
        u = jnp.maximum(jnp.dot(h, wup_ref[:, f * FF_BLOCK:(f + 1) * FF_BLOCK],
                                preferred_element_type=F32), 0.0)
        d = jnp.dot((u * u).astype(BF16), wdown_ref[f * FF_BLOCK:(f + 1) * FF_BLOCK, :],
                    preferred_element_type=F32)
        acc = d if acc is None else acc + d
    x = x + acc

    h = _rmsnorm_rows(x, nple_ref[...]).astype(BF16)
    ple = jnp.dot(p_ref[...].astype(BF16), wpp_ref[...], preferred_element_type=F32)
    x = x + _sigmoid(jnp.dot(h, wpg_ref[...], preferred_element_type=F32)) * ple
    if final_norm:
        x = _rmsnorm_rows(x, nfin_ref[...])
    o_ref[...] = x


def _const_spec(shape):
    zeros = (0,) * len(shape)
    return pl.BlockSpec(shape, lambda *_: zeros, pipeline_mode=pl.Buffered(1))


def _ones_block_diag():
    i = np.arange(QUAD_W) // HEAD_W
    return jnp.asarray((i[:, None] == i[None, :]).astype(np.float32), dtype=BF16)


def _chunk_tri(rows, chunk):
    i = np.arange(rows)
    m = (i[:, None] >= i[None, :]) & ((i[:, None] // chunk) == (i[None, :] // chunk))
    return jnp.asarray(m.astype(np.float32), dtype=BF16)


def _lane_vec(v, first_lane):
    out = jnp.zeros((1, SMALL_W), F32)
    return out.at[0, first_lane:first_lane + v.shape[0]].set(v.astype(F32))


def _retention_tables(S):
    half = HEAD_W // 2
    inv_freq = ROPE_BASE ** (-jnp.linspace(0.0, 1.0, half, dtype=F32))
    ang = jnp.arange(S, dtype=F32)[:, None] * inv_freq[None, :]
    cos = jnp.tile(jnp.cos(ang), (1, 2 * N_HEADS))
    sin = jnp.sin(ang)
    sin_signed = jnp.tile(jnp.concatenate([-sin, sin], axis=1), (1, N_HEADS))
    L = RET_CHUNK
    log_gamma = jnp.log1p(-jnp.exp2(-5.0 - jnp.arange(N_HEADS, dtype=F32)))
    idx = jnp.arange(L, dtype=F32)
    tril = jnp.tril(jnp.ones((L, L), bool))
    dmat = jnp.exp(jnp.where(tril, (idx[:, None] - idx[None, :]) * log_gamma[:, None, None], -jnp.inf))
    dmat = dmat.reshape(2, 4, L, L).transpose(0, 2, 1, 3).reshape(2, L, 4 * L)
    kdec = jnp.repeat(jnp.exp((L - 1 - idx)[:, None] * log_gamma[None, :]), HEAD_W, axis=1)
    qdec = jnp.repeat(jnp.exp((idx + 1)[:, None] * log_gamma[None, :]), HEAD_W, axis=1)
    cgam = jnp.repeat(jnp.exp(L * log_gamma)[None, :], HEAD_W, axis=1)
    return cos, sin_signed, dmat, qdec, kdec, cgam


def kernel(x, p, norm_mix, w_in, gdn_conv_w, gdn_A_log, gdn_dt_bias, gdn_norm_w, ssm_conv_w,
           ssm_conv_b, ssm_A_log, ssm_dt_bias, ssm_D, ssm_norm_w, w_branch, w_out, norm_mlp,
           w_up, w_down, norm_ple, w_ple_gate, w_ple_proj, norm_final):
    B, S, D = x.shape
    depth = w_in.shape[0]
    assert D == D_MODEL and S % TILE_MIX == 0 and B % min(B, MIX_ROWS) == 0 and (B * S) % TILE_POST == 0
    ones_bd = _ones_block_diag()
    tri_gdn = _chunk_tri(128, GDN_CHUNK)
    tri_ssd = _chunk_tri(128, SSM_CHUNK)
    cos, sin_signed, dmat, qdec, kdec, cgam = _retention_tables(S)
    row = lambda v: v.reshape(1, -1).astype(F32)

    sizes = (3 * BRANCH_W, BRANCH_W, N_HEADS, N_HEADS, BRANCH_W, BRANCH_W + 2 * SSM_GROUPS * SSM_STATE,
             N_HEADS, BRANCH_W, BRANCH_W, BRANCH_W, BRANCH_W, 3 * D_MODEL)
    offs = np.concatenate([[0], np.cumsum(sizes)])
    seg = lambda w, i: w[:, offs[i]:offs[i + 1]]

    for i in range(depth):
        w = w_in[i].astype(BF16)
        nmix = row(norm_mix[i])
        pad = jnp.zeros((D, SMALL_W - 2 * N_HEADS), BF16)
        w_mix = jnp.concatenate(
            [seg(w, 0), seg(w, 1), seg(w, 2), seg(w, 3), pad, seg(w, 4), seg(w, 5), seg(w, 6),
             jnp.zeros((D, SMALL_W - N_HEADS), BF16), w[:, offs[7]:offs[11]]], axis=1)
        consts_a = (gdn_conv_w[i].astype(F32), _lane_vec(gdn_A_log[i], N_HEADS),
                    _lane_vec(gdn_dt_bias[i], N_HEADS), row(jnp.tile(gdn_norm_w[i], N_HEADS)), tri_gdn)
        consts_b = (ssm_conv_w[i].astype(F32), row(ssm_conv_b[i]), _lane_vec(ssm_A_log[i], 0),
                    _lane_vec(ssm_dt_bias[i], 0), row(jnp.repeat(ssm_D[i], HEAD_W)), row(ssm_norm_w[i]),
                    tri_ssd)
        consts_c = (dmat, qdec, kdec, cgam, ones_bd)
        nb = min(B, MIX_ROWS)
        pos_spec = pl.BlockSpec((TILE_MIX, BRANCH_W), lambda g, j: (j, 0))
        y_spec = pl.BlockSpec((nb, TILE_MIX, BRANCH_W), lambda g, j: (g, j, 0))
        y_shape = jax.ShapeDtypeStruct((B, S, BRANCH_W), BF16)
        xbc_w = BRANCH_W + 2 * SSM_GROUPS * SSM_STATE
        y_a, y_b, y_c = pl.pallas_call(
            functools.partial(_mixers_kernel, nb=nb),
            out_shape=(y_shape, y_shape, y_shape),
            grid=(B // nb, S // TILE_MIX),
            in_specs=[pl.BlockSpec((nb, TILE_MIX, D), lambda g, j: (g, j, 0)), _const_spec(nmix.shape),
                      _const_spec(w_mix.shape)]
            + [_const_spec(c.shape) for c in consts_a + consts_b] + [pos_spec, pos_spec]
            + [_const_spec(c.shape) for c in consts_c],
            out_specs=(y_spec, y_spec, y_spec),
            scratch_shapes=[pltpu.VMEM((nb, CONV_PAD, 3 * BRANCH_W), F32),
                            pltpu.VMEM((nb, 2, HEAD_W, QUAD_W), F32),
                            pltpu.VMEM((nb, CONV_PAD, xbc_w), F32),
                            pltpu.VMEM((nb, SSM_GROUPS, 128, QUAD_W), F32),
                            pltpu.VMEM((nb, 2, HEAD_W, QUAD_W), F32)],
            compiler_params=pltpu.CompilerParams(
                dimension_semantics=("arbitrary", "arbitrary"), vmem_limit_bytes=VMEM_LIMIT_MIXER),
            name="mixers",
        )(x, nmix, w_mix, *consts_a, *consts_b, cos, sin_signed, *consts_c)
        n_tok = B * S
        tok = lambda a: a.reshape(n_tok, a.shape[-1])
        consts = (nmix, seg(w, 11), w_branch[i].astype(BF16), w_out[i].astype(BF16), row(norm_mlp[i]),
                  w_up[i].astype(BF16), w_down[i].astype(BF16), row(norm_ple[i]),
                  w_ple_gate[i].astype(BF16), w_ple_proj[i].astype(BF16), row(norm_final))
        tile_spec = lambda width: pl.BlockSpec((TILE_POST, width), lambda t: (t, 0))
        x = pl.pallas_call(
            functools.partial(_post_kernel, final_norm=(i == depth - 1)),
            out_shape=jax.ShapeDtypeStruct((n_tok, D), F32),
            grid=(n_tok // TILE_POST,),
            in_specs=[tile_spec(D), tile_spec(BRANCH_W), tile_spec(BRANCH_W), tile_spec(BRANCH_W),
                      tile_spec(PLE_DIM)] + [_const_spec(c.shape) for c in consts],
            out_specs=tile_spec(D),
            compiler_params=pltpu.CompilerParams(
                dimension_semantics=("arbitrary",), vmem_limit_bytes=VMEM_LIMIT_POST),
            name="merge_mlp_ple",
        )(tok(x), tok(y_a), tok(y_b), tok(y_c), tok(p[i]), *consts).reshape(B, S, D)
    return x
```

```python
import functools
import math

import numpy as np
import jax
import jax.numpy as jnp
from jax import lax
from jax.experimental import pallas as pl
from jax.experimental.pallas import tpu as pltpu

F32 = jnp.float32
BF16 = jnp.bfloat16

D_MODEL = 1024
PLE_DIM = 256
D_FF = 4 * D_MODEL
NORM_EPS = 1e-6
N_HEADS = 8
HEAD_W = 64
BRANCH_W = N_HEADS * HEAD_W
QUAD_W = 4 * HEAD_W
SMALL_W = 128
GDN_CHUNK = 64
SSM_CHUNK = 128
RET_CHUNK = 128
SSM_GROUPS = 2
SSM_STATE = 64
ROPE_BASE = 10000.0
CONV_K = 4
CONV_PAD = 8

TILE_MIX = 128
MIX_ROWS = 4
TILE_POST = 512
FF_BLOCK = 1024
VMEM_LIMIT_MIXER = 56 * 1024 * 1024
VMEM_LIMIT_POST = 56 * 1024 * 1024


def _rmsnorm_rows(x, w):
    y = x * lax.rsqrt(jnp.mean(x * x, axis=-1, keepdims=True) + NORM_EPS)
    return y * w


def _sigmoid(x):
    return 1.0 / (1.0 + jnp.exp(-x))


def _silu(x):
    hx = 0.5 * x
    return hx + hx * jnp.tanh(hx)


def _softplus(x):
    return jnp.maximum(x, 0.0) + jnp.log1p(jnp.exp(-jnp.abs(x)))


def _dot(a, b):
    return jnp.dot(a.astype(BF16), b.astype(BF16), preferred_element_type=F32)


def _dot_nt(a, b):
    return lax.dot_general(a.astype(BF16), b.astype(BF16), (((1,), (1,)), ((), ())),
                           preferred_element_type=F32)


def _dot_tn(a, b):
    return lax.dot_general(a.astype(BF16), b.astype(BF16), (((0,), (0,)), ((), ())),
                           preferred_element_type=F32)


def _chunk_cumsum(x, tri_ref):
    x1 = x.astype(BF16)
    x2 = (x - x1.astype(F32)).astype(BF16)
    tri = tri_ref[...]
    outs = []
    for s in range(x.shape[0] // 128):
        sl = slice(s * 128, (s + 1) * 128)
        outs.append(jnp.dot(tri, x1[sl], preferred_element_type=F32)
                    + jnp.dot(tri, x2[sl], preferred_element_type=F32))
    return jnp.concatenate(outs, axis=0)


def _iota(shape, dim):
    return lax.broadcasted_iota(jnp.int32, shape, dim)


def _block_diag_rows(y):
    yb = y.astype(BF16)
    half = yb.shape[1] // 2
    zero = jnp.zeros((yb.shape[0], half), BF16)
    low = _iota(zero.shape, 1) < HEAD_W
    blocks = []
    for e in range(4):
        tile = yb[:, (e // 2) * half:(e // 2 + 1) * half]
        kept = jnp.where(low if e % 2 == 0 else jnp.logical_not(low), tile, zero)
        blocks.append(jnp.concatenate([kept, zero] if e < 2 else [zero, kept], axis=1))
    return jnp.concatenate(blocks, axis=0)


def _expand_heads(cols, first_lane, rows):
    low = _iota((rows, 128), 1) < HEAD_W
    parts = []
    for p in range(N_HEADS // 2):
        a = jnp.broadcast_to(cols[:, first_lane + 2 * p:first_lane + 2 * p + 1], (rows, 128))
        b = jnp.broadcast_to(cols[:, first_lane + 2 * p + 1:first_lane + 2 * p + 2], (rows, 128))
        parts.append(jnp.where(low, a, b))
    return jnp.concatenate(parts, axis=1)


def _causal_conv(pre, carry_ref, w_ref, rows):
    cat = jnp.concatenate([carry_ref[...], pre], axis=0)
    acc = pre * w_ref[CONV_K - 1:CONV_K, :]
    for s in range(1, CONV_K):
        shifted = pltpu.roll(cat, s, 0)[CONV_PAD:CONV_PAD + rows, :]
        acc = acc + shifted * w_ref[CONV_K - 1 - s:CONV_K - s, :]
    carry_ref[...] = pre[rows - CONV_PAD:rows, :]
    return acc


def _head_outer(a, b):
    low = _iota((HEAD_W, 128), 1) < HEAD_W
    outs = []
    for p in range(2):
        m = _dot_tn(a[:, p * 128:(p + 1) * 128], b[:, p * 128:(p + 1) * 128])
        outs.append(jnp.where(low, m[0:HEAD_W], m[HEAD_W:2 * HEAD_W]))
    return jnp.concatenate(outs, axis=1)


def _head_sum(sq, ones_bd_ref):
    ones = ones_bd_ref[...]
    return jnp.concatenate(
        [jnp.dot(sq[:, i:i + QUAD_W].astype(BF16), ones, preferred_element_type=F32)
         for i in range(0, sq.shape[1], QUAD_W)], axis=1)


def _inv_unit_lower_quads(mats):
    shape = mats[0].shape
    r = _iota(shape, 0)
    c = _iota(shape, 1) % HEAD_W
    eye = jnp.where(r == c, 1.0, 0.0).astype(F32)
    in16 = (r // 16) == (c // 16)
    in32 = (r // 32) == (c // 32)
    off16 = jnp.logical_and(in32, jnp.logical_not(in16))

    def mm(xs, ys):
        return [jnp.dot(x.astype(BF16), _block_diag_rows(y), preferred_element_type=F32)
                for x, y in zip(xs, ys)]

    def mm2(xs, ps):
        outs = mm([jnp.concatenate([x, p], axis=0) for x, p in zip(xs, ps)], ps)
        n = shape[0]
        return [o[0:n] for o in outs], [o[n:2 * n] for o in outs]

    ns = [jnp.where(in16, -a, 0.0) for a in mats]
    xs = [eye + n for n in ns]
    ps = mm(ns, ns)
    yield
    for _ in range(2):
        xp, ps = mm2(xs, ps)
        xs = [x + d for x, d in zip(xs, xp)]
        yield
    xs = [x + d for x, d in zip(xs, mm(xs, ps))]
    yield
    for keep in (off16, jnp.logical_not(in32)):
        ls = [jnp.where(keep, a, 0.0) for a in mats]
        ts = mm(xs, ls)
        yield
        xs = [x - d for x, d in zip(xs, mm(ts, xs))]
        yield
    return xs


def _gdn_pieces(h, wqkv_ref, wz_ref, wsm_ref, convw_ref, alog_ref, dtb_ref, gnw_ref,
                ones_bd_ref, tri_ref, o_ref, cbuf_ref, state_ref, nb):
    T = TILE_MIX
    L = GDN_CHUNK
    n_chunks = T // L
    R = nb * T
    W = BRANCH_W

    def project(gi):
        return jnp.dot(h, wqkv_ref[:, gi * W:(gi + 1) * W], preferred_element_type=F32)

    def conv_silu(pre, gi):
        conv = jnp.concatenate(
            [_causal_conv(pre[b * T:(b + 1) * T], cbuf_ref.at[b, :, gi * W:(gi + 1) * W],
                          convw_ref.at[:, gi * W:(gi + 1) * W], T) for b in range(nb)], axis=0)
        return _silu(conv)

    pre_q = project(0)
    yield
    pre_k = project(1)
    yield
    q = conv_silu(pre_q, 0)
    yield
    pre_v = project(2)
    yield
    k = conv_silu(pre_k, 1)
    yield
    z = jnp.dot(h, wz_ref[...], preferred_element_type=F32)
    yield
    v = conv_silu(pre_v, 2)
    yield
    q = q * lax.rsqrt(_head_sum(q * q, ones_bd_ref) + 1e-6) * (HEAD_W ** -0.5)
    yield
    k = k * lax.rsqrt(_head_sum(k * k, ones_bd_ref) + 1e-6)
    yield

    small = jnp.dot(h, wsm_ref[...], preferred_element_type=F32)
    beta = _sigmoid(small)
    glog = -jnp.exp(alog_ref[...]) * _softplus(small + dtb_ref[...])
    gcum = _chunk_cumsum(glog, tri_ref)
    beta_x = _expand_heads(beta, 0, R)
    g_x = _expand_heads(gcum, N_HEADS, R)
    yield
    kb = k * beta_x
    vb = v * beta_x
    eg = jnp.exp(g_x)
    kbg = kb * eg
    qd = q * eg
    yield

    shape = (L, QUAD_W)
    r = _iota(shape, 0)
    c = _iota(shape, 1) % HEAD_W
    diag = r == c
    lower = r >= c
    strict = r > c

    def piece(t, b, ci, u):
        r0 = b * T + ci * L
        return t[r0:r0 + L, u * QUAD_W:(u + 1) * QUAD_W]

    items = [(b, ci, u) for ci in range(n_chunks) for b in range(nb) for u in range(2)]
    a_mat, attn = {}, {}
    for ci in range(n_chunks):
        its = [it for it in items if it[1] == ci]
        scs = [_dot_nt(jnp.concatenate([piece(kb, *it), piece(q, *it)], axis=0),
                       _block_diag_rows(piece(k, *it))) for it in its]
        for it, sc in zip(its, scs):
            gq = piece(g_x, *it)
            g_row = jnp.sum(jnp.where(diag, gq, 0.0), axis=0, keepdims=True)
            dec = jnp.exp(jnp.where(lower, gq - g_row, -jnp.inf))
            a_mat[it] = jnp.where(strict, sc[0:L] * dec, 0.0)
            attn[it] = (sc[L:2 * L] * dec).astype(BF16)
        yield
    tinv = yield from _inv_unit_lower_quads([a_mat[it] for it in items])
    uw = {}
    for it, ti in zip(items, tinv):
        rhs = jnp.concatenate([_block_diag_rows(piece(vb, *it)),
                               _block_diag_rows(piece(kbg, *it))], axis=1)
        uw[it] = jnp.dot(ti.astype(BF16), rhs, preferred_element_type=F32)
    yield

    chains = [(b, u) for b in range(nb) for u in range(2)]
    states = {ch: state_ref[ch[0], ch[1]] for ch in chains}
    outs = {}
    for ci in range(n_chunks):
        wq = {(b, u): jnp.dot(
            jnp.concatenate([uw[(b, ci, u)][:, QUAD_W:], piece(qd, b, ci, u)], axis=0).astype(BF16),
            _block_diag_rows(states[(b, u)]), preferred_element_type=F32) for b, u in chains}
        yield
        v_new = {(b, u): uw[(b, ci, u)][:, 0:QUAD_W] - wq[(b, u)][0:L] for b, u in chains}
        for b, u in chains:
            gq = piece(g_x, b, ci, u)
            g_last = gq[L - 1:L, :]
            upd = _head_outer(piece(k, b, ci, u) * jnp.exp(g_last - gq), v_new[(b, u)])
            states[(b, u)] = states[(b, u)] * jnp.exp(g_last) + upd
        yield
        for b, u in chains:
            outs[(b, ci, u)] = wq[(b, u)][L:2 * L] + jnp.dot(
                attn[(b, ci, u)], _block_diag_rows(v_new[(b, u)]), preferred_element_type=F32)
        yield
    for b, u in chains:
        state_ref[b, u] = states[(b, u)]

    o_all = jnp.concatenate(
        [jnp.concatenate([outs[(b, ci, 0)], outs[(b, ci, 1)]], axis=1)
         for b in range(nb) for ci in range(n_chunks)], axis=0)
    y = o_all * lax.rsqrt(_head_sum(o_all * o_all, ones_bd_ref) * (1.0 / HEAD_W) + NORM_EPS)
    y = y * gnw_ref[...] * _silu(z)
    o_ref[...] = y.reshape(nb, T, BRANCH_W).astype(o_ref.dtype)
    yield


def _ssd_pieces(h, wz_ref, wxbc_ref, wsm_ref, convw_ref, convb_ref, alog_ref, dtb_ref,
                dskip_ref, gnw_ref, tri_ref, o_ref, cbuf_ref, state_ref, nb):
    T = TILE_MIX
    L = SSM_CHUNK
    assert T == L
    R = nb * T
    C = BRANCH_W + 2 * SSM_GROUPS * SSM_STATE

    pre = jnp.dot(h, wxbc_ref[...], preferred_element_type=F32)
    yield
    z = jnp.dot(h, wz_ref[...], preferred_element_type=F32)
    yield
    half = C // 2
    parts = []
    for ci in range(2):
        sl = slice(ci * half, (ci + 1) * half)
        conv = jnp.concatenate(
            [_causal_conv(pre[b * T:(b + 1) * T, sl], cbuf_ref.at[b, :, sl], convw_ref.at[:, sl], T)
             for b in range(nb)], axis=0)
        parts.append(_silu(conv + convb_ref[:, sl]))
        yield
    xbc = jnp.concatenate(parts, axis=1)
    xs = xbc[:, 0:BRANCH_W]
    b_pair = xbc[:, BRANCH_W:BRANCH_W + 128]
    c_pair = xbc[:, BRANCH_W + 128:BRANCH_W + 256]
    small = jnp.dot(h, wsm_ref[...], preferred_element_type=F32)
    dt = _softplus(small + dtb_ref[...])
    a = -jnp.exp(alog_ref[...]) * dt
    acs = _chunk_cumsum(a, tri_ref)
    dt_x = _expand_heads(dt, 0, R)
    acs_x = _expand_heads(acs, 0, R)
    xdt = xs * dt_x
    eacs_x = jnp.exp(acs_x)
    yield

    r = _iota((L, L), 0)
    c = _iota((L, L), 1)
    diag = r == c
    lower = r >= c
    lane_low = _iota((L, 128), 1) < SSM_STATE

    items = [(b, g) for b in range(nb) for g in range(SSM_GROUPS)]

    def rows(t, b):
        return t[b * T:(b + 1) * T]

    def keep(t, g):
        return jnp.where(lane_low if g == 0 else jnp.logical_not(lane_low), t, 0.0)

    c_g = {(b, g): keep(rows(c_pair, b), g) for b, g in items}
    cb = {(b, g): _dot_nt(c_g[(b, g)], rows(b_pair, b)) for b, g in items}
    y_off = {(b, g): _dot(c_g[(b, g)], state_ref[b, g]) for b, g in items}
    yield
    ys = {}
    for b in range(nb):
        for g in range(SSM_GROUPS):
            parts = []
            for e in range(4):
                hd = 4 * g + e
                col = jnp.broadcast_to(rows(acs, b)[:, hd:hd + 1], (L, L))
                row = jnp.sum(jnp.where(diag, col, 0.0), axis=0, keepdims=True)
                parts.append(cb[(b, g)] * jnp.exp(jnp.where(lower, col - row, -jnp.inf)))
            m_quad = jnp.concatenate(parts, axis=1).astype(BF16)
            sl = slice(g * QUAD_W, (g + 1) * QUAD_W)
            xq = rows(xdt, b)[:, sl]
            acs_q = rows(acs_x, b)[:, sl]
            last_q = acs_q[L - 1:L, :]
            y_diag = jnp.dot(m_quad, _block_diag_rows(xq), preferred_element_type=F32)
            states = _dot_tn(keep(rows(b_pair, b), g), xq * jnp.exp(last_q - acs_q))
            state_ref[b, g] = state_ref[b, g] * jnp.exp(last_q) + states
            ys[(b, g)] = y_diag + y_off[(b, g)] * rows(eacs_x, b)[:, sl]
        yield
    y = jnp.concatenate([jnp.concatenate([ys[(b, 0)], ys[(b, 1)]], axis=1) for b in range(nb)], axis=0)
    y = (y + dskip_ref[...] * xs) * _silu(z)
    outs = []
    for g in range(SSM_GROUPS):
        yg = y[:, g * QUAD_W:(g + 1) * QUAD_W]
        outs.append(yg * lax.rsqrt(jnp.mean(yg * yg, axis=-1, keepdims=True) + NORM_EPS))
    y = jnp.concatenate(outs, axis=1) * gnw_ref[...]
    o_ref[...] = y.reshape(nb, T, BRANCH_W).astype(o_ref.dtype)
    yield


def _rotate_pairs(t, cos, sin_signed):
    n = t.shape[1]
    lo = (_iota(t.shape, 1) % HEAD_W) < (HEAD_W // 2)
    partner = jnp.where(lo, pltpu.roll(t, n - HEAD_W // 2, 1), pltpu.roll(t, HEAD_W // 2, 1))
    return t * cos + partner * sin_signed


def _ret_pieces(h, w_ref, cos_ref, sin_ref, dmat_ref, qdec_ref, kdec_ref, cgam_ref,
                ones_bd_ref, o_ref, state_ref, nb):
    T = TILE_MIX
    L = RET_CHUNK
    assert T == L
    W = BRANCH_W

    proj = []
    for gi in range(4):
        proj.append(jnp.dot(h, w_ref[:, gi * W:(gi + 1) * W], preferred_element_type=F32))
        yield
    cos = jnp.concatenate([cos_ref[...]] * nb, axis=0)
    sin = jnp.concatenate([sin_ref[...]] * nb, axis=0)
    q = _rotate_pairs(proj[0], cos, sin)
    yield
    k = _rotate_pairs(proj[1], cos, sin) * (HEAD_W ** -0.5)
    yield
    v = proj[2]
    gate = proj[3]

    items = [(b, u) for b in range(nb) for u in range(2)]

    def piece(t, b, u):
        return t[b * T:(b + 1) * T, u * QUAD_W:(u + 1) * QUAD_W]

    ys = {}
    for b0 in range(0, nb, 2):
        its = [it for it in items if b0 <= it[0] < b0 + 2]
        sc = {it: _dot_nt(piece(q, *it), _block_diag_rows(piece(k, *it))) for it in its}
        y_off = {}
        for b, u in its:
            sl = slice(u * QUAD_W, (u + 1) * QUAD_W)
            y_off[(b, u)] = jnp.dot((piece(q, b, u) * qdec_ref[:, sl]).astype(BF16),
                                    _block_diag_rows(state_ref[b, u]), preferred_element_type=F32)
        yield
        for b, u in its:
            sl = slice(u * QUAD_W, (u + 1) * QUAD_W)
            a = (sc[(b, u)] * dmat_ref[u]).astype(BF16)
            ys[(b, u)] = y_off[(b, u)] + jnp.dot(a, _block_diag_rows(piece(v, b, u)),
                                                 preferred_element_type=F32)
            kv = _head_outer(piece(k, b, u) * kdec_ref[:, sl], piece(v, b, u))
            state_ref[b, u] = state_ref[b, u] * cgam_ref[:, sl] + kv
        yield
    y = jnp.concatenate([jnp.concatenate([ys[(b, 0)], ys[(b, 1)]], axis=1) for b in range(nb)], axis=0)
    y = y * lax.rsqrt(_head_sum(y * y, ones_bd_ref) * (1.0 / HEAD_W) + NORM_EPS) * _silu(gate)
    o_ref[...] = y.reshape(nb, T, BRANCH_W).astype(o_ref.dtype)
    yield


_MIX_COLS = tuple(np.cumsum([0, 3 * BRANCH_W, BRANCH_W, SMALL_W, BRANCH_W,
                             BRANCH_W + 2 * SSM_GROUPS * SSM_STATE, SMALL_W, 4 * BRANCH_W]).tolist())

_PIECE_ORDER = ("AAAAAAA" "CACACACA" "BABA" "ABACABACABABABABBABB" "ACACBACACBACACBCC" "ABC")


def _mixers_kernel(x_ref, nw_ref, w_ref,
                   a_convw, a_alog, a_dtb, a_gnw, a_tri,
                   b_convw, b_convb, b_alog, b_dtb, b_dskip, b_gnw, b_tri,
                   c_cos, c_sin, c_dmat, c_qdec, c_kdec, c_cgam, ones_bd_ref,
                   oa_ref, ob_ref, oc_ref,
                   a_cbuf, a_state, b_cbuf, b_state, c_state, *, nb):
    R = nb * TILE_MIX

    @pl.when(pl.program_id(1) == 0)
    def _():
        for ref in (a_state, b_state, c_state, a_cbuf, b_cbuf):
            ref[...] = jnp.zeros_like(ref)

    h = _rmsnorm_rows(x_ref[...].reshape(R, D_MODEL), nw_ref[...]).astype(BF16)
    a_wqkv, a_wz, a_wsm, b_wz, b_wxbc, b_wsm, c_w = (
        w_ref.at[:, lo:hi] for lo, hi in zip(_MIX_COLS[:-1], _MIX_COLS[1:]))
    gens = {
        "A": _gdn_pieces(h, a_wqkv, a_wz, a_wsm, a_convw, a_alog, a_dtb, a_gnw, ones_bd_ref,
                         a_tri, oa_ref, a_cbuf, a_state, nb),
        "B": _ssd_pieces(h, b_wz, b_wxbc, b_wsm, b_convw, b_convb, b_alog, b_dtb, b_dskip,
                         b_gnw, b_tri, ob_ref, b_cbuf, b_state, nb),
        "C": _ret_pieces(h, c_w, c_cos, c_sin, c_dmat, c_qdec, c_kdec, c_cgam, ones_bd_ref,
                         oc_ref, c_state, nb),
    }
    for tag in _PIECE_ORDER:
        if tag in gens and next(gens[tag], "done") == "done":
            del gens[tag]
    for gen in gens.values():
        for _ in gen:
            pass


def _post_kernel(x_ref, ya_ref, yb_ref, yc_ref, p_ref, nmix_ref, wgate_ref, wbr_ref, wout_ref,
                 nmlp_ref, wup_ref, wdown_ref, nple_ref, wpg_ref, wpp_ref, nfin_ref, o_ref,
                 *, final_norm):
    x = x_ref[...]
    h = _rmsnorm_rows(x, nmix_ref[...]).astype(BF16)
    mixed = None
    for n, y_ref in enumerate((ya_ref, yb_ref, yc_ref)):
        logits = jnp.dot(h, wgate_ref[:, n * D_MODEL:(n + 1) * D_MODEL], preferred_element_type=F32)
        branch = jnp.dot(y_ref[...], wbr_ref[n], preferred_element_type=F32)
        term = _sigmoid(logits) * branch
        mixed = term if mixed is None else mixed + term
    x = x + jnp.dot(mixed.astype(BF16), wout_ref[...], preferred_element_type=F32)

    h = _rmsnorm_rows(x, nmlp_ref[...]).astype(BF16)
    acc = None
    for f in range(D_FF // FF_BLOCK):
        u = jnp.maximum(jnp.dot(h, wup_ref[:, f * FF_BLOCK:(f + 1) * FF_BLOCK],
                                preferred_element_type=F32), 0.0)
        d = jnp.dot((u * u).astype(BF16), wdown_ref[f * FF_BLOCK:(f + 1) * FF_BLOCK, :],
                    preferred_element_type=F32)
        acc = d if acc is None else acc + d
    x = x + acc

    h = _rmsnorm_rows(x, nple_ref[...]).astype(BF16)
    ple = jnp.dot(p_ref[...].astype(BF16), wpp_ref[...], preferred_element_type=F32)
    x = x + _sigmoid(jnp.dot(h, wpg_ref[...], preferred_element_type=F32)) * ple
    if final_norm:
        x = _rmsnorm_rows(x, nfin_ref[...])
    o_ref[...] = x


_SRC = {"a_small": 4 * BRANCH_W, "b": 4 * BRANCH_W + 2 * N_HEADS}
_SRC["b_small"] = _SRC["b"] + 2 * BRANCH_W + 2 * SSM_GROUPS * SSM_STATE
_SRC["c"] = _SRC["b_small"] + N_HEADS
MIX_SRC_W = 5504
PREP_STEPS = 8


def _prep_kernel(win_ref, wbr_ref, wout_ref, wup_ref, wdown_ref, wpg_ref, wpp_ref,
                 mix_ref, obr_ref, oout_ref, oup_ref, odown_ref, opg_ref, opp_ref):
    for src_ref, dst_ref in ((wbr_ref, obr_ref), (wout_ref, oout_ref), (wup_ref, oup_ref),
                             (wdown_ref, odown_ref), (wpg_ref, opg_ref), (wpp_ref, opp_ref)):
        dst_ref[...] = src_ref[...].astype(dst_ref.dtype)

    w = win_ref[...].astype(F32)
    lane = _iota((w.shape[0], 128), 1)

    def put(dst_lo, width, src_lo, keep=None):
        lo = src_lo // 128 * 128
        hi = -(-(src_lo + width) // 128) * 128
        t = w[:, lo:hi]
        if src_lo != lo:
            t = pltpu.roll(t, (hi - lo) - (src_lo - lo), 1)
        t = t[:, 0:width]
        if keep is not None:
            t = jnp.where(lane < keep, t, 0.0)
        mix_ref[:, dst_lo:dst_lo + width] = t.astype(mix_ref.dtype)

    c = _MIX_COLS
    put(c[0], c[2] - c[0], 0)
    put(c[2], SMALL_W, _SRC["a_small"], keep=2 * N_HEADS)
    put(c[3], c[5] - c[3], _SRC["b"])
    put(c[5], SMALL_W, _SRC["b_small"], keep=N_HEADS)
    put(c[6], c[7] - c[6], _SRC["c"])


def _prepare_weights(w_in, w_branch, w_out, w_up, w_down, w_ple_gate, w_ple_proj):
    depth = w_in.shape[0]
    srcs = (w_in, w_branch.reshape(depth, -1, D_MODEL), w_out, w_up, w_down, w_ple_gate, w_ple_proj)
    widths = (MIX_SRC_W,) + tuple(a.shape[2] for a in srcs[1:])
    in_specs = [pl.BlockSpec((None, a.shape[1] // PREP_STEPS, wd), lambda l, r: (l, r, 0))
                for a, wd in zip(srcs, widths)]
    out_widths = (_MIX_COLS[-1],) + widths[1:]
    out_shape = [jax.ShapeDtypeStruct((depth, a.shape[1], wd), BF16) for a, wd in zip(srcs, out_widths)]
    out_specs = [pl.BlockSpec((None, a.shape[1] // PREP_STEPS, wd), lambda l, r: (l, r, 0))
                 for a, wd in zip(srcs, out_widths)]
    return pl.pallas_call(
        _prep_kernel, out_shape=out_shape, grid=(depth, PREP_STEPS),
        in_specs=in_specs, out_specs=out_specs,
        compiler_params=pltpu.CompilerParams(
            dimension_semantics=("arbitrary", "arbitrary"), vmem_limit_bytes=VMEM_LIMIT_POST),
        name="prepare_weights",
    )(*srcs)


def _const_spec(shape):
    zeros = (0,) * len(shape)
    return pl.BlockSpec(shape, lambda *_: zeros, pipeline_mode=pl.Buffered(1))


def _layer_spec(stacked, layer):
    zeros = (0,) * (stacked.ndim - 1)
    return pl.BlockSpec((None,) + stacked.shape[1:], lambda *_: (layer,) + zeros,
                        pipeline_mode=pl.Buffered(1))


def _ones_block_diag():
    i = np.arange(QUAD_W) // HEAD_W
    return jnp.asarray((i[:, None] == i[None, :]).astype(np.float32), dtype=BF16)


def _chunk_tri(rows, chunk):
    i = np.arange(rows)
    m = (i[:, None] >= i[None, :]) & ((i[:, None] // chunk) == (i[None, :] // chunk))
    return jnp.asarray(m.astype(np.float32), dtype=BF16)


def _lane_vec(v, first_lane):
    out = jnp.zeros((1, SMALL_W), F32)
    return out.at[0, first_lane:first_lane + v.shape[0]].set(v.astype(F32))


def _retention_tables(S):
    half = HEAD_W // 2
    inv_freq = ROPE_BASE ** (-jnp.linspace(0.0, 1.0, half, dtype=F32))
    ang = jnp.arange(S, dtype=F32)[:, None] * inv_freq[None, :]
    cos = jnp.tile(jnp.cos(ang), (1, 2 * N_HEADS))
    sin = jnp.sin(ang)
    sin_signed = jnp.tile(jnp.concatenate([-sin, sin], axis=1), (1, N_HEADS))
    L = RET_CHUNK
    log_gamma = jnp.log1p(-jnp.exp2(-5.0 - jnp.arange(N_HEADS, dtype=F32)))
    idx = jnp.arange(L, dtype=F32)
    tril = jnp.tril(jnp.ones((L, L), bool))
    dmat = jnp.exp(jnp.where(tril, (idx[:, None] - idx[None, :]) * log_gamma[:, None, None], -jnp.inf))
    dmat = dmat.reshape(2, 4, L, L).transpose(0, 2, 1, 3).reshape(2, L, 4 * L)
    kdec = jnp.repeat(jnp.exp((L - 1 - idx)[:, None] * log_gamma[None, :]), HEAD_W, axis=1)
    qdec = jnp.repeat(jnp.exp((idx + 1)[:, None] * log_gamma[None, :]), HEAD_W, axis=1)
    cgam = jnp.repeat(jnp.exp(L * log_gamma)[None, :], HEAD_W, axis=1)
    return cos, sin_signed, dmat, qdec, kdec, cgam


def kernel(x, p, norm_mix, w_in, gdn_conv_w, gdn_A_log, gdn_dt_bias, gdn_norm_w, ssm_conv_w,
           ssm_conv_b, ssm_A_log, ssm_dt_bias, ssm_D, ssm_norm_w, w_branch, w_out, norm_mlp,
           w_up, w_down, norm_ple, w_ple_gate, w_ple_proj, norm_final):
    B, S, D = x.shape
    depth = w_in.shape[0]
    assert D == D_MODEL and S % TILE_MIX == 0 and B % min(B, MIX_ROWS) == 0 and (B * S) % TILE_POST == 0
    ones_bd = _ones_block_diag()
    tri_gdn = _chunk_tri(128, GDN_CHUNK)
    tri_ssd = _chunk_tri(128, SSM_CHUNK)
    cos, sin_signed, dmat, qdec, kdec, cgam = _retention_tables(S)
    row = lambda v: v.reshape(1, -1).astype(F32)

    sizes = (3 * BRANCH_W, BRANCH_W, N_HEADS, N_HEADS, BRANCH_W, BRANCH_W + 2 * SSM_GROUPS * SSM_STATE,
             N_HEADS, BRANCH_W, BRANCH_W, BRANCH_W, BRANCH_W, 3 * D_MODEL)
    offs = np.concatenate([[0], np.cumsum(sizes)])

    w_in16 = w_in.astype(BF16)
    w_gate = w_in16[:, :, offs[11]:offs[12]]
    w_mix, wb_branch, wb_out, wb_up, wb_down, wb_pg, wb_pp = _prepare_weights(
        w_in16, w_branch, w_out, w_up, w_down, w_ple_gate, w_ple_proj)
    wb_branch = wb_branch.reshape(depth, w_branch.shape[1], w_branch.shape[2], D)

    for i in range(depth):
        nmix = row(norm_mix[i])
        consts_a = (gdn_conv_w[i].astype(F32), _lane_vec(gdn_A_log[i], N_HEADS),
                    _lane_vec(gdn_dt_bias[i], N_HEADS), row(jnp.tile(gdn_norm_w[i], N_HEADS)), tri_gdn)
        consts_b = (ssm_conv_w[i].astype(F32), row(ssm_conv_b[i]), _lane_vec(ssm_A_log[i], 0),
                    _lane_vec(ssm_dt_bias[i], 0), row(jnp.repeat(ssm_D[i], HEAD_W)), row(ssm_norm_w[i]),
                    tri_ssd)
        consts_c = (dmat, qdec, kdec, cgam, ones_bd)
        nb = min(B, MIX_ROWS)
        pos_spec = pl.BlockSpec((TILE_MIX, BRANCH_W), lambda g, j: (j, 0))
        y_spec = pl.BlockSpec((nb, TILE_MIX, BRANCH_W), lambda g, j: (g, j, 0))
        y_shape = jax.ShapeDtypeStruct((B, S, BRANCH_W), BF16)
        xbc_w = BRANCH_W + 2 * SSM_GROUPS * SSM_STATE
        y_a, y_b, y_c = pl.pallas_call(
            functools.partial(_mixers_kernel, nb=nb),
            out_shape=(y_shape, y_shape, y_shape),
            grid=(B // nb, S // TILE_MIX),
            in_specs=[pl.BlockSpec((nb, TILE_MIX, D), lambda g, j: (g, j, 0)), _const_spec(nmix.shape),
                      _layer_spec(w_mix, i)]
            + [_const_spec(c.shape) for c in consts_a + consts_b] + [pos_spec, pos_spec]
            + [_const_spec(c.shape) for c in consts_c],
            out_specs=(y_spec, y_spec, y_spec),
            scratch_shapes=[pltpu.VMEM((nb, CONV_PAD, 3 * BRANCH_W), F32),
                            pltpu.VMEM((nb, 2, HEAD_W, QUAD_W), F32),
                            pltpu.VMEM((nb, CONV_PAD, xbc_w), F32),
                            pltpu.VMEM((nb, SSM_GROUPS, 128, QUAD_W), F32),
                            pltpu.VMEM((nb, 2, HEAD_W, QUAD_W), F32)],
            compiler_params=pltpu.CompilerParams(
                dimension_semantics=("arbitrary", "arbitrary"), vmem_limit_bytes=VMEM_LIMIT_MIXER),
            name="mixers",
        )(x, nmix, w_mix, *consts_a, *consts_b, cos, sin_signed, *consts_c)
        n_tok = B * S
        tok = lambda a: a.reshape(n_tok, a.shape[-1])
        consts = (nmix, w_gate, wb_branch, wb_out, row(norm_mlp[i]), wb_up, wb_down,
                  row(norm_ple[i]), wb_pg, wb_pp, row(norm_final))
        stacked = (1, 2, 3, 5, 6, 8, 9)
        tile_spec = lambda width: pl.BlockSpec((TILE_POST, width), lambda t: (t, 0))
        p_spec = pl.BlockSpec((None, TILE_POST, PLE_DIM), lambda t, layer=i: (layer, t, 0))
        x = pl.pallas_call(
            functools.partial(_post_kernel, final_norm=(i == depth - 1)),
            out_shape=jax.ShapeDtypeStruct((n_tok, D), F32),
            grid=(n_tok // TILE_POST,),
            in_specs=[tile_spec(D), tile_spec(BRANCH_W), tile_spec(BRANCH_W), tile_spec(BRANCH_W),
                      p_spec] + [_layer_spec(c, i) if n in stacked else _const_spec(c.shape)
                                 for n, c in enumerate(consts)],
            out_specs=tile_spec(D),
            compiler_params=pltpu.CompilerParams(
                dimension_semantics=("arbitrary",), vmem_limit_bytes=VMEM_LIMIT_POST),
            name="merge_mlp_ple",
        )(tok(x), tok(y_a), tok(y_b), tok(y_c), p.reshape(depth, n_tok, PLE_DIM), *consts).reshape(B, S, D)
    return x
```

```python
import functools

import numpy as np
import jax
import jax.numpy as jnp
from jax import lax
from jax.experimental import pallas as pl
from jax.experimental.pallas import tpu as pltpu

F32 = jnp.float32
BF16 = jnp.bfloat16

D_MODEL = 1024
PLE_DIM = 256
D_FF = 4 * D_MODEL
NORM_EPS = 1e-6
N_HEADS = 8
HEAD_W = 64
BRANCH_W = N_HEADS * HEAD_W
QUAD_W = 4 * HEAD_W
SMALL_W = 128
GDN_CHUNK = 64
SSM_CHUNK = 128
RET_CHUNK = 128
SSM_GROUPS = 2
SSM_STATE = 64
ROPE_BASE = 10000.0
CONV_K = 4
CONV_PAD = 8
LANE = 128

TILE_MIX = 128
MIX_ROWS = 4
TILE_POST = 512
FF_BLOCK = 1024
VMEM_LIMIT_MIXER = 56 * 1024 * 1024
VMEM_LIMIT_POST = 56 * 1024 * 1024


def _rmsnorm_rows(x, w):
    y = x * lax.rsqrt(jnp.mean(x * x, axis=-1, keepdims=True) + NORM_EPS)
    return y * w


def _sigmoid(x):
    return 1.0 / (1.0 + jnp.exp(-x))


def _silu(x):
    hx = 0.5 * x
    return hx + hx * jnp.tanh(hx)


def _softplus(x):
    return jnp.maximum(x, 0.0) + jnp.log1p(jnp.exp(-jnp.abs(x)))


def _dot(a, b):
    return jnp.dot(a.astype(BF16), b.astype(BF16), preferred_element_type=F32)


def _dot_nt(a, b):
    return lax.dot_general(a.astype(BF16), b.astype(BF16), (((1,), (1,)), ((), ())),
                           preferred_element_type=F32)


def _dot_tn(a, b):
    return lax.dot_general(a.astype(BF16), b.astype(BF16), (((0,), (0,)), ((), ())),
                           preferred_element_type=F32)


def _chunk_cumsum(x, tri_ref):
    x1 = x.astype(BF16)
    x2 = (x - x1.astype(F32)).astype(BF16)
    tri = tri_ref[...]
    outs = []
    for s in range(x.shape[0] // 128):
        sl = slice(s * 128, (s + 1) * 128)
        outs.append(jnp.dot(tri, x1[sl], preferred_element_type=F32)
                    + jnp.dot(tri, x2[sl], preferred_element_type=F32))
    return jnp.concatenate(outs, axis=0)


def _iota(shape, dim):
    return lax.broadcasted_iota(jnp.int32, shape, dim)


def _block_diag_rows(y):
    yb = y.astype(BF16)
    half = yb.shape[1] // 2
    zero = jnp.zeros((yb.shape[0], half), BF16)
    low = _iota(zero.shape, 1) < HEAD_W
    blocks = []
    for e in range(4):
        tile = yb[:, (e // 2) * half:(e // 2 + 1) * half]
        kept = jnp.where(low if e % 2 == 0 else jnp.logical_not(low), tile, zero)
        blocks.append(jnp.concatenate([kept, zero] if e < 2 else [zero, kept], axis=1))
    return jnp.concatenate(blocks, axis=0)


def _expand_heads(cols, first_lane, rows):
    low = _iota((rows, 128), 1) < HEAD_W
    parts = []
    for p in range(N_HEADS // 2):
        a = jnp.broadcast_to(cols[:, first_lane + 2 * p:first_lane + 2 * p + 1], (rows, 128))
        b = jnp.broadcast_to(cols[:, first_lane + 2 * p + 1:first_lane + 2 * p + 2], (rows, 128))
        parts.append(jnp.where(low, a, b))
    return jnp.concatenate(parts, axis=1)


def _causal_conv(pre, carry_ref, w_ref, rows):
    cat = jnp.concatenate([carry_ref[...], pre], axis=0)
    acc = pre * w_ref[CONV_K - 1:CONV_K, :]
    for s in range(1, CONV_K):
        shifted = pltpu.roll(cat, s, 0)[CONV_PAD:CONV_PAD + rows, :]
        acc = acc + shifted * w_ref[CONV_K - 1 - s:CONV_K - s, :]
    carry_ref[...] = pre[rows - CONV_PAD:rows, :]
    return acc


def _head_outer(a, b):
    low = _iota((HEAD_W, 128), 1) < HEAD_W
    outs = []
    for p in range(2):
        m = _dot_tn(a[:, p * 128:(p + 1) * 128], b[:, p * 128:(p + 1) * 128])
        outs.append(jnp.where(low, m[0:HEAD_W], m[HEAD_W:2 * HEAD_W]))
    return jnp.concatenate(outs, axis=1)


def _head_sum(sq, ones_bd_ref):
    ones = ones_bd_ref[...]
    return jnp.concatenate(
        [jnp.dot(sq[:, i:i + QUAD_W].astype(BF16), ones, preferred_element_type=F32)
         for i in range(0, sq.shape[1], QUAD_W)], axis=1)


def _inv_unit_lower_quads(mats):
    shape = mats[0].shape
    r = _iota(shape, 0)
    c = _iota(shape, 1) % HEAD_W
    eye = jnp.where(r == c, 1.0, 0.0).astype(F32)
    in16 = (r // 16) == (c // 16)
    in32 = (r // 32) == (c // 32)
    off16 = jnp.logical_and(in32, jnp.logical_not(in16))

    def mm(xs, ys):
        return [jnp.dot(x.astype(BF16), _block_diag_rows(y), preferred_element_type=F32)
                for x, y in zip(xs, ys)]

    def mm2(xs, ps):
        outs = mm([jnp.concatenate([x, p], axis=0) for x, p in zip(xs, ps)], ps)
        n = shape[0]
        return [o[0:n] for o in outs], [o[n:2 * n] for o in outs]

    ns = [jnp.where(in16, -a, 0.0) for a in mats]
    xs = [eye + n for n in ns]
    ps = mm(ns, ns)
    yield
    for _ in range(2):
        xp, ps = mm2(xs, ps)
        xs = [x + d for x, d in zip(xs, xp)]
        yield
    xs = [x + d for x, d in zip(xs, mm(xs, ps))]
    yield
    for keep in (off16, jnp.logical_not(in32)):
        ls = [jnp.where(keep, a, 0.0) for a in mats]
        ts = mm(xs, ls)
        yield
        xs = [x - d for x, d in zip(xs, mm(ts, xs))]
        yield
    return xs


def _gdn_pieces(h, wqkv_ref, wz_ref, wsm_ref, convw_ref, alog_ref, dtb_ref, gnw_ref,
                ones_bd_ref, tri_ref, o_ref, cbuf_ref, state_ref, nb):
    T = TILE_MIX
    L = GDN_CHUNK
    n_chunks = T // L
    R = nb * T
    W = BRANCH_W

    def project(gi):
        return jnp.dot(h, wqkv_ref[:, gi * W:(gi + 1) * W], preferred_element_type=F32)

    def conv_silu(pre, gi):
        conv = jnp.concatenate(
            [_causal_conv(pre[b * T:(b + 1) * T], cbuf_ref.at[b, :, gi * W:(gi + 1) * W],
                          convw_ref.at[:, gi * W:(gi + 1) * W], T) for b in range(nb)], axis=0)
        return _silu(conv)

    pre_q = project(0)
    yield
    pre_k = project(1)
    yield
    q = conv_silu(pre_q, 0)
    yield
    pre_v = project(2)
    yield
    k = conv_silu(pre_k, 1)
    yield
    z = jnp.dot(h, wz_ref[...], preferred_element_type=F32)
    yield
    v = conv_silu(pre_v, 2)
    yield
    q = q * lax.rsqrt(_head_sum(q * q, ones_bd_ref) + 1e-6) * (HEAD_W ** -0.5)
    yield
    k = k * lax.rsqrt(_head_sum(k * k, ones_bd_ref) + 1e-6)
    yield

    small = jnp.dot(h, wsm_ref[...], preferred_element_type=F32)
    beta = _sigmoid(small)
    glog = -jnp.exp(alog_ref[...]) * _softplus(small + dtb_ref[...])
    gcum = _chunk_cumsum(glog, tri_ref)
    beta_x = _expand_heads(beta, 0, R)
    g_x = _expand_heads(gcum, N_HEADS, R)
    yield
    kb = k * beta_x
    vb = v * beta_x
    eg = jnp.exp(g_x)
    kbg = kb * eg
    qd = q * eg
    yield

    shape = (L, QUAD_W)
    r = _iota(shape, 0)
    c = _iota(shape, 1) % HEAD_W
    diag = r == c
    lower = r >= c
    strict = r > c

    def piece(t, b, ci, u):
        r0 = b * T + ci * L
        return t[r0:r0 + L, u * QUAD_W:(u + 1) * QUAD_W]

    items = [(b, ci, u) for ci in range(n_chunks) for b in range(nb) for u in range(2)]
    a_mat, attn = {}, {}
    for ci in range(n_chunks):
        its = [it for it in items if it[1] == ci]
        scs = [_dot_nt(jnp.concatenate([piece(kb, *it), piece(q, *it)], axis=0),
                       _block_diag_rows(piece(k, *it))) for it in its]
        for it, sc in zip(its, scs):
            gq = piece(g_x, *it)
            g_row = jnp.sum(jnp.where(diag, gq, 0.0), axis=0, keepdims=True)
            dec = jnp.exp(jnp.where(lower, gq - g_row, -jnp.inf))
            a_mat[it] = jnp.where(strict, sc[0:L] * dec, 0.0)
            attn[it] = (sc[L:2 * L] * dec).astype(BF16)
        yield
    tinv = yield from _inv_unit_lower_quads([a_mat[it] for it in items])
    uw = {}
    for it, ti in zip(items, tinv):
        rhs = jnp.concatenate([_block_diag_rows(piece(vb, *it)),
                               _block_diag_rows(piece(kbg, *it))], axis=1)
        uw[it] = jnp.dot(ti.astype(BF16), rhs, preferred_element_type=F32)
    yield

    chains = [(b, u) for b in range(nb) for u in range(2)]
    states = {ch: state_ref[ch[0], ch[1]] for ch in chains}
    outs = {}
    for ci in range(n_chunks):
        wq = {(b, u): jnp.dot(
            jnp.concatenate([uw[(b, ci, u)][:, QUAD_W:], piece(qd, b, ci, u)], axis=0).astype(BF16),
            _block_diag_rows(states[(b, u)]), preferred_element_type=F32) for b, u in chains}
        yield
        v_new = {(b, u): uw[(b, ci, u)][:, 0:QUAD_W] - wq[(b, u)][0:L] for b, u in chains}
        for b, u in chains:
            gq = piece(g_x, b, ci, u)
            g_last = gq[L - 1:L, :]
            upd = _head_outer(piece(k, b, ci, u) * jnp.exp(g_last - gq), v_new[(b, u)])
            states[(b, u)] = states[(b, u)] * jnp.exp(g_last) + upd
        yield
        for b, u in chains:
            outs[(b, ci, u)] = wq[(b, u)][L:2 * L] + jnp.dot(
                attn[(b, ci, u)], _block_diag_rows(v_new[(b, u)]), preferred_element_type=F32)
        yield
    for b, u in chains:
        state_ref[b, u] = states[(b, u)]

    o_all = jnp.concatenate(
        [jnp.concatenate([outs[(b, ci, 0)], outs[(b, ci, 1)]], axis=1)
         for b in range(nb) for ci in range(n_chunks)], axis=0)
    y = o_all * lax.rsqrt(_head_sum(o_all * o_all, ones_bd_ref) * (1.0 / HEAD_W) + NORM_EPS)
    y = y * gnw_ref[...] * _silu(z)
    o_ref[...] = y.reshape(nb, T, BRANCH_W).astype(o_ref.dtype)
    yield


def _ssd_pieces(h, wz_ref, wxbc_ref, wsm_ref, convw_ref, convb_ref, alog_ref, dtb_ref,
                dskip_ref, gnw_ref, tri_ref, o_ref, cbuf_ref, state_ref, nb):
    T = TILE_MIX
    L = SSM_CHUNK
    assert T == L
    R = nb * T
    C = BRANCH_W + 2 * SSM_GROUPS * SSM_STATE

    pre = jnp.dot(h, wxbc_ref[...], preferred_element_type=F32)
    yield
    z = jnp.dot(h, wz_ref[...], preferred_element_type=F32)
    yield
    half = C // 2
    parts = []
    for ci in range(2):
        sl = slice(ci * half, (ci + 1) * half)
        conv = jnp.concatenate(
            [_causal_conv(pre[b * T:(b + 1) * T, sl], cbuf_ref.at[b, :, sl], convw_ref.at[:, sl], T)
             for b in range(nb)], axis=0)
        parts.append(_silu(conv + convb_ref[:, sl]))
        yield
    xbc = jnp.concatenate(parts, axis=1)
    xs = xbc[:, 0:BRANCH_W]
    b_pair = xbc[:, BRANCH_W:BRANCH_W + 128]
    c_pair = xbc[:, BRANCH_W + 128:BRANCH_W + 256]
    small = jnp.dot(h, wsm_ref[...], preferred_element_type=F32)
    dt = _softplus(small + dtb_ref[...])
    a = -jnp.exp(alog_ref[...]) * dt
    acs = _chunk_cumsum(a, tri_ref)
    dt_x = _expand_heads(dt, 0, R)
    acs_x = _expand_heads(acs, 0, R)
    xdt = xs * dt_x
    eacs_x = jnp.exp(acs_x)
    yield

    r = _iota((L, L), 0)
    c = _iota((L, L), 1)
    diag = r == c
    lower = r >= c
    lane_low = _iota((L, 128), 1) < SSM_STATE

    items = [(b, g) for b in range(nb) for g in range(SSM_GROUPS)]

    def rows(t, b):
        return t[b * T:(b + 1) * T]

    def keep(t, g):
        return jnp.where(lane_low if g == 0 else jnp.logical_not(lane_low), t, 0.0)

    c_g = {(b, g): keep(rows(c_pair, b), g) for b, g in items}
    cb = {(b, g): _dot_nt(c_g[(b, g)], rows(b_pair, b)) for b, g in items}
    y_off = {(b, g): _dot(c_g[(b, g)], state_ref[b, g]) for b, g in items}
    yield
    ys = {}
    for b in range(nb):
        for g in range(SSM_GROUPS):
            parts = []
            for e in range(4):
                hd = 4 * g + e
                col = jnp.broadcast_to(rows(acs, b)[:, hd:hd + 1], (L, L))
                row = jnp.sum(jnp.where(diag, col, 0.0), axis=0, keepdims=True)
                parts.append(cb[(b, g)] * jnp.exp(jnp.where(lower, col - row, -jnp.inf)))
            m_quad = jnp.concatenate(parts, axis=1).astype(BF16)
            sl = slice(g * QUAD_W, (g + 1) * QUAD_W)
            xq = rows(xdt, b)[:, sl]
            acs_q = rows(acs_x, b)[:, sl]
            last_q = acs_q[L - 1:L, :]
            y_diag = jnp.dot(m_quad, _block_diag_rows(xq), preferred_element_type=F32)
            states = _dot_tn(keep(rows(b_pair, b), g), xq * jnp.exp(last_q - acs_q))
            state_ref[b, g] = state_ref[b, g] * jnp.exp(last_q) + states
            ys[(b, g)] = y_diag + y_off[(b, g)] * rows(eacs_x, b)[:, sl]
        yield
    y = jnp.concatenate([jnp.concatenate([ys[(b, 0)], ys[(b, 1)]], axis=1) for b in range(nb)], axis=0)
    y = (y + dskip_ref[...] * xs) * _silu(z)
    outs = []
    for g in range(SSM_GROUPS):
        yg = y[:, g * QUAD_W:(g + 1) * QUAD_W]
        outs.append(yg * lax.rsqrt(jnp.mean(yg * yg, axis=-1, keepdims=True) + NORM_EPS))
    y = jnp.concatenate(outs, axis=1) * gnw_ref[...]
    o_ref[...] = y.reshape(nb, T, BRANCH_W).astype(o_ref.dtype)
    yield


def _rotate_pairs(t, cos, sin_signed):
    n = t.shape[1]
    lo = (_iota(t.shape, 1) % HEAD_W) < (HEAD_W // 2)
    partner = jnp.where(lo, pltpu.roll(t, n - HEAD_W // 2, 1), pltpu.roll(t, HEAD_W // 2, 1))
    return t * cos + partner * sin_signed


def _ret_pieces(h, w_ref, cos_ref, sin_ref, dmat_ref, qdec_ref, kdec_ref, cgam_ref,
                ones_bd_ref, o_ref, state_ref, nb):
    T = TILE_MIX
    L = RET_CHUNK
    assert T == L
    W = BRANCH_W

    proj = []
    for gi in range(4):
        proj.append(jnp.dot(h, w_ref[:, gi * W:(gi + 1) * W], preferred_element_type=F32))
        yield
    cos = jnp.concatenate([jnp.concatenate([cos_ref[...]] * (W // LANE), axis=1)] * nb, axis=0)
    sin = jnp.concatenate([jnp.concatenate([sin_ref[...]] * (W // LANE), axis=1)] * nb, axis=0)
    q = _rotate_pairs(proj[0], cos, sin)
    yield
    k = _rotate_pairs(proj[1], cos, sin) * (HEAD_W ** -0.5)
    yield
    v = proj[2]
    gate = proj[3]

    items = [(b, u) for b in range(nb) for u in range(2)]

    def piece(t, b, u):
        return t[b * T:(b + 1) * T, u * QUAD_W:(u + 1) * QUAD_W]

    ys = {}
    for b0 in range(0, nb, 2):
        its = [it for it in items if b0 <= it[0] < b0 + 2]
        sc = {it: _dot_nt(piece(q, *it), _block_diag_rows(piece(k, *it))) for it in its}
        y_off = {}
        for b, u in its:
            sl = slice(u * QUAD_W, (u + 1) * QUAD_W)
            y_off[(b, u)] = jnp.dot((piece(q, b, u) * qdec_ref[:, sl]).astype(BF16),
                                    _block_diag_rows(state_ref[b, u]), preferred_element_type=F32)
        yield
        for b, u in its:
            sl = slice(u * QUAD_W, (u + 1) * QUAD_W)
            a = (sc[(b, u)] * dmat_ref[u]).astype(BF16)
            ys[(b, u)] = y_off[(b, u)] + jnp.dot(a, _block_diag_rows(piece(v, b, u)),
                                                 preferred_element_type=F32)
            kv = _head_outer(piece(k, b, u) * kdec_ref[:, sl], piece(v, b, u))
            state_ref[b, u] = state_ref[b, u] * cgam_ref[:, sl] + kv
        yield
    y = jnp.concatenate([jnp.concatenate([ys[(b, 0)], ys[(b, 1)]], axis=1) for b in range(nb)], axis=0)
    y = y * lax.rsqrt(_head_sum(y * y, ones_bd_ref) * (1.0 / HEAD_W) + NORM_EPS) * _silu(gate)
    o_ref[...] = y.reshape(nb, T, BRANCH_W).astype(o_ref.dtype)
    yield


_MIX_COLS = tuple(np.cumsum([0, 3 * BRANCH_W, BRANCH_W, SMALL_W, BRANCH_W,
                             BRANCH_W + 2 * SSM_GROUPS * SSM_STATE, SMALL_W, 4 * BRANCH_W]).tolist())

_PIECE_ORDER = ("AAAAAAA" "CACACACA" "BABA" "ABACABACABABABABBABB" "ACACBACACBACACBCC" "ABC")


def _mixers_kernel(x_ref, nw_ref, w_ref,
                   a_convw, a_alog, a_dtb, a_gnw, a_tri,
                   b_convw, b_convb, b_alog, b_dtb, b_dskip, b_gnw, b_tri,
                   c_cos, c_sin, c_dmat, c_qdec, c_kdec, c_cgam, ones_bd_ref,
                   oa_ref, ob_ref, oc_ref,
                   a_cbuf, a_state, b_cbuf, b_state, c_state, *, nb):
    R = nb * TILE_MIX

    @pl.when(pl.program_id(1) == 0)
    def _():
        for ref in (a_state, b_state, c_state, a_cbuf, b_cbuf):
            ref[...] = jnp.zeros_like(ref)

    h = _rmsnorm_rows(x_ref[...].reshape(R, D_MODEL), nw_ref[...]).astype(BF16)
    a_wqkv, a_wz, a_wsm, b_wz, b_wxbc, b_wsm, c_w = (
        w_ref.at[:, lo:hi] for lo, hi in zip(_MIX_COLS[:-1], _MIX_COLS[1:]))
    gens = {
        "A": _gdn_pieces(h, a_wqkv, a_wz, a_wsm, a_convw, a_alog, a_dtb, a_gnw, ones_bd_ref,
                         a_tri, oa_ref, a_cbuf, a_state, nb),
        "B": _ssd_pieces(h, b_wz, b_wxbc, b_wsm, b_convw, b_convb, b_alog, b_dtb, b_dskip,
                         b_gnw, b_tri, ob_ref, b_cbuf, b_state, nb),
        "C": _ret_pieces(h, c_w, c_cos, c_sin, c_dmat, c_qdec, c_kdec, c_cgam, ones_bd_ref,
                         oc_ref, c_state, nb),
    }
    for tag in _PIECE_ORDER:
        if tag in gens and next(gens[tag], "done") == "done":
            del gens[tag]
    for gen in gens.values():
        for _ in gen:
            pass


def _post_kernel(x_ref, ya_ref, yb_ref, yc_ref, p_ref, nmix_ref, wgate_ref, wbr_ref, wout_ref,
                 nmlp_ref, wup_ref, wdown_ref, nple_ref, wpg_ref, wpp_ref, nfin_ref, o_ref,
                 *, final_norm):
    x = x_ref[...]
    h = _rmsnorm_rows(x, nmix_ref[...]).astype(BF16)
    mixed = None
    for n, y_ref in enumerate((ya_ref, yb_ref, yc_ref)):
        logits = jnp.dot(h, wgate_ref[:, n * D_MODEL:(n + 1) * D_MODEL], preferred_element_type=F32)
        branch = jnp.dot(y_ref[...], wbr_ref[n], preferred_element_type=F32)
        term = _sigmoid(logits) * branch
        mixed = term if mixed is None else mixed + term
    x = x + jnp.dot(mixed.astype(BF16), wout_ref[...], preferred_element_type=F32)

    h = _rmsnorm_rows(x, nmlp_ref[...]).astype(BF16)
    acc = None
    for f in range(D_FF // FF_BLOCK):
        u = jnp.maximum(jnp.dot(h, wup_ref[:, f * FF_BLOCK:(f + 1) * FF_BLOCK],
                                preferred_element_type=F32), 0.0)
        d = jnp.dot((u * u).astype(BF16), wdown_ref[f * FF_BLOCK:(f + 1) * FF_BLOCK, :],
                    preferred_element_type=F32)
        acc = d if acc is None else acc + d
    x = x + acc

    h = _rmsnorm_rows(x, nple_ref[...]).astype(BF16)
    ple = jnp.dot(p_ref[...].astype(BF16), wpp_ref[...], preferred_element_type=F32)
    x = x + _sigmoid(jnp.dot(h, wpg_ref[...], preferred_element_type=F32)) * ple
    if final_norm:
        x = _rmsnorm_rows(x, nfin_ref[...])
    o_ref[...] = x


_SRC = {"a_small": 4 * BRANCH_W, "b": 4 * BRANCH_W + 2 * N_HEADS}
_SRC["b_small"] = _SRC["b"] + 2 * BRANCH_W + 2 * SSM_GROUPS * SSM_STATE
_SRC["c"] = _SRC["b_small"] + N_HEADS
MIX_SRC_W = -(-(_SRC["c"] + 4 * BRANCH_W) // LANE) * LANE
PREP_STEPS = 8


def _prep_kernel(win_ref, wbr_ref, wout_ref, wup_ref, wdown_ref, wpg_ref, wpp_ref,
                 mix_ref, obr_ref, oout_ref, oup_ref, odown_ref, opg_ref, opp_ref):
    for src_ref, dst_ref in ((wbr_ref, obr_ref), (wout_ref, oout_ref), (wup_ref, oup_ref),
                             (wdown_ref, odown_ref), (wpg_ref, opg_ref), (wpp_ref, opp_ref)):
        dst_ref[...] = src_ref[...].astype(dst_ref.dtype)

    w = win_ref[...].astype(F32)
    lane = _iota((w.shape[0], 128), 1)

    def put(dst_lo, width, src_lo, keep=None):
        lo = src_lo // 128 * 128
        hi = -(-(src_lo + width) // 128) * 128
        t = w[:, lo:hi]
        if src_lo != lo:
            t = pltpu.roll(t, (hi - lo) - (src_lo - lo), 1)
        t = t[:, 0:width]
        if keep is not None:
            t = jnp.where(lane < keep, t, 0.0)
        mix_ref[:, dst_lo:dst_lo + width] = t.astype(mix_ref.dtype)

    c = _MIX_COLS
    put(c[0], c[2] - c[0], 0)
    put(c[2], SMALL_W, _SRC["a_small"], keep=2 * N_HEADS)
    put(c[3], c[5] - c[3], _SRC["b"])
    put(c[5], SMALL_W, _SRC["b_small"], keep=N_HEADS)
    put(c[6], c[7] - c[6], _SRC["c"])


def _prepare_weights(w_in, w_branch, w_out, w_up, w_down, w_ple_gate, w_ple_proj):
    depth = w_in.shape[0]
    srcs = (w_in, w_branch.reshape(depth, -1, D_MODEL), w_out, w_up, w_down, w_ple_gate, w_ple_proj)
    widths = (MIX_SRC_W,) + tuple(a.shape[2] for a in srcs[1:])
    in_specs = [pl.BlockSpec((None, a.shape[1] // PREP_STEPS, wd), lambda l, r: (l, r, 0))
                for a, wd in zip(srcs, widths)]
    out_widths = (_MIX_COLS[-1],) + widths[1:]
    out_shape = [jax.ShapeDtypeStruct((depth, a.shape[1], wd), BF16) for a, wd in zip(srcs, out_widths)]
    out_specs = [pl.BlockSpec((None, a.shape[1] // PREP_STEPS, wd), lambda l, r: (l, r, 0))
                 for a, wd in zip(srcs, out_widths)]
    return pl.pallas_call(
        _prep_kernel, out_shape=out_shape, grid=(depth, PREP_STEPS),
        in_specs=in_specs, out_specs=out_specs,
        compiler_params=pltpu.CompilerParams(
            dimension_semantics=("arbitrary", "arbitrary"), vmem_limit_bytes=VMEM_LIMIT_POST),
        name="prepare_weights",
    )(*srcs)


def _const_spec(shape):
    zeros = (0,) * len(shape)
    return pl.BlockSpec(shape, lambda *_: zeros, pipeline_mode=pl.Buffered(1))


def _layer_spec(stacked, layer):
    zeros = (0,) * (stacked.ndim - 1)
    return pl.BlockSpec((None,) + stacked.shape[1:], lambda *_: (layer,) + zeros,
                        pipeline_mode=pl.Buffered(1))


def _ones_block_diag():
    i = np.arange(QUAD_W) // HEAD_W
    return jnp.asarray((i[:, None] == i[None, :]).astype(np.float32), dtype=BF16)


def _chunk_tri(rows, chunk):
    i = np.arange(rows)
    m = (i[:, None] >= i[None, :]) & ((i[:, None] // chunk) == (i[None, :] // chunk))
    return jnp.asarray(m.astype(np.float32), dtype=BF16)


def _lane_vec(v, first_lane):
    out = jnp.zeros((1, SMALL_W), F32)
    return out.at[0, first_lane:first_lane + v.shape[0]].set(v.astype(F32))


def _retention_tables(S):
    half = HEAD_W // 2
    inv_freq = ROPE_BASE ** (-jnp.linspace(0.0, 1.0, half, dtype=F32))
    ang = jnp.arange(S, dtype=F32)[:, None] * inv_freq[None, :]
    cos = jnp.tile(jnp.cos(ang), (1, LANE // half))
    sin = jnp.sin(ang)
    sin_signed = jnp.tile(jnp.concatenate([-sin, sin], axis=1), (1, LANE // HEAD_W))
    L = RET_CHUNK
    log_gamma = jnp.log1p(-jnp.exp2(-5.0 - jnp.arange(N_HEADS, dtype=F32)))
    idx = jnp.arange(L, dtype=F32)
    tril = jnp.tril(jnp.ones((L, L), bool))
    dmat = jnp.exp(jnp.where(tril, (idx[:, None] - idx[None, :]) * log_gamma[:, None, None], -jnp.inf))
    dmat = dmat.reshape(2, 4, L, L).transpose(0, 2, 1, 3).reshape(2, L, 4 * L)
    kdec = jnp.repeat(jnp.exp((L - 1 - idx)[:, None] * log_gamma[None, :]), HEAD_W, axis=1)
    qdec = jnp.repeat(jnp.exp((idx + 1)[:, None] * log_gamma[None, :]), HEAD_W, axis=1)
    cgam = jnp.repeat(jnp.exp(L * log_gamma)[None, :], HEAD_W, axis=1)
    return cos, sin_signed, dmat, qdec, kdec, cgam


def kernel(x, p, norm_mix, w_in, gdn_conv_w, gdn_A_log, gdn_dt_bias, gdn_norm_w, ssm_conv_w,
           ssm_conv_b, ssm_A_log, ssm_dt_bias, ssm_D, ssm_norm_w, w_branch, w_out, norm_mlp,
           w_up, w_down, norm_ple, w_ple_gate, w_ple_proj, norm_final):
    B, S, D = x.shape
    depth = w_in.shape[0]
    assert D == D_MODEL and S % TILE_MIX == 0 and B % min(B, MIX_ROWS) == 0 and (B * S) % TILE_POST == 0
    ones_bd = _ones_block_diag()
    tri_gdn = _chunk_tri(128, GDN_CHUNK)
    tri_ssd = _chunk_tri(128, SSM_CHUNK)
    cos, sin_signed, dmat, qdec, kdec, cgam = _retention_tables(S)
    row = lambda v: v.reshape(1, -1).astype(F32)

    sizes = (3 * BRANCH_W, BRANCH_W, N_HEADS, N_HEADS, BRANCH_W, BRANCH_W + 2 * SSM_GROUPS * SSM_STATE,
             N_HEADS, BRANCH_W, BRANCH_W, BRANCH_W, BRANCH_W, 3 * D_MODEL)
    offs = np.concatenate([[0], np.cumsum(sizes)])

    w_in16 = w_in.astype(BF16)
    w_gate = w_in16[:, :, offs[11]:offs[12]]
    w_mix, wb_branch, wb_out, wb_up, wb_down, wb_pg, wb_pp = _prepare_weights(
        w_in16, w_branch, w_out, w_up, w_down, w_ple_gate, w_ple_proj)
    wb_branch = wb_branch.reshape(depth, w_branch.shape[1], w_branch.shape[2], D)

    for i in range(depth):
        nmix = row(norm_mix[i])
        consts_a = (gdn_conv_w[i].astype(F32), _lane_vec(gdn_A_log[i], N_HEADS),
                    _lane_vec(gdn_dt_bias[i], N_HEADS), row(jnp.tile(gdn_norm_w[i], N_HEADS)), tri_gdn)
        consts_b = (ssm_conv_w[i].astype(F32), row(ssm_conv_b[i]), _lane_vec(ssm_A_log[i], 0),
                    _lane_vec(ssm_dt_bias[i], 0), row(jnp.repeat(ssm_D[i], HEAD_W)), row(ssm_norm_w[i]),
                    tri_ssd)
        consts_c = (dmat, qdec, kdec, cgam, ones_bd)
        nb = min(B, MIX_ROWS)
        pos_spec = pl.BlockSpec((TILE_MIX, LANE), lambda g, j: (j, 0))
        y_spec = pl.BlockSpec((nb, TILE_MIX, BRANCH_W), lambda g, j: (g, j, 0))
        y_shape = jax.ShapeDtypeStruct((B, S, BRANCH_W), BF16)
        xbc_w = BRANCH_W + 2 * SSM_GROUPS * SSM_STATE
        y_a, y_b, y_c = pl.pallas_call(
            functools.partial(_mixers_kernel, nb=nb),
            out_shape=(y_shape, y_shape, y_shape),
            grid=(B // nb, S // TILE_MIX),
            in_specs=[pl.BlockSpec((nb, TILE_MIX, D), lambda g, j: (g, j, 0)), _const_spec(nmix.shape),
                      _layer_spec(w_mix, i)]
            + [_const_spec(c.shape) for c in consts_a + consts_b] + [pos_spec, pos_spec]
            + [_const_spec(c.shape) for c in consts_c],
            out_specs=(y_spec, y_spec, y_spec),
            scratch_shapes=[pltpu.VMEM((nb, CONV_PAD, 3 * BRANCH_W), F32),
                            pltpu.VMEM((nb, 2, HEAD_W, QUAD_W), F32),
                            pltpu.VMEM((nb, CONV_PAD, xbc_w), F32),
                            pltpu.VMEM((nb, SSM_GROUPS, 128, QUAD_W), F32),
                            pltpu.VMEM((nb, 2, HEAD_W, QUAD_W), F32)],
            compiler_params=pltpu.CompilerParams(
                dimension_semantics=("arbitrary", "arbitrary"), vmem_limit_bytes=VMEM_LIMIT_MIXER),
            name="mixers",
        )(x, nmix, w_mix, *consts_a, *consts_b, cos, sin_signed, *consts_c)
        n_tok = B * S
        tok = lambda a: a.reshape(n_tok, a.shape[-1])
        consts = (nmix, w_gate, wb_branch, wb_out, row(norm_mlp[i]), wb_up, wb_down,
                  row(norm_ple[i]), wb_pg, wb_pp, row(norm_final))
        stacked = (1, 2, 3, 5, 6, 8, 9)
        tile_spec = lambda width: pl.BlockSpec((TILE_POST, width), lambda t: (t, 0))
        p_spec = pl.BlockSpec((None, TILE_POST, PLE_DIM), lambda t, layer=i: (layer, t, 0))
        x = pl.pallas_call(
            functools.partial(_post_kernel, final_norm=(i == depth - 1)),
            out_shape=jax.ShapeDtypeStruct((n_tok, D), F32),
            grid=(n_tok // TILE_POST,),
            in_specs=[tile_spec(D), tile_spec(BRANCH_W), tile_spec(BRANCH_W), tile_spec(BRANCH_W),
                      p_spec] + [_layer_spec(c, i) if n in stacked else _const_spec(c.shape)
                                 for n, c in enumerate(consts)],
            out_specs=tile_spec(D),
            compiler_params=pltpu.CompilerParams(
                dimension_semantics=("arbitrary",), vmem_limit_bytes=VMEM_LIMIT_POST),
            name="merge_mlp_ple",
        )(tok(x), tok(y_a), tok(y_b), tok(y_c), p.reshape(depth, n_tok, PLE_DIM), *consts).reshape(B, S, D)
    return x
```

```python
import functools

import numpy as np
import jax
import jax.numpy as jnp
from jax import lax
from jax.experimental import pallas as pl
from jax.experimental.pallas import tpu as pltpu

F32 = jnp.float32
BF16 = jnp.bfloat16

D_MODEL = 1024
PLE_DIM = 256
D_FF = 4 * D_MODEL
NORM_EPS = 1e-6
N_HEADS = 8
HEAD_W = 64
BRANCH_W = N_HEADS * HEAD_W
QUAD_W = 4 * HEAD_W
SMALL_W = 128
GDN_CHUNK = 64
SSM_CHUNK = 128
RET_CHUNK = 128
SSM_GROUPS = 2
SSM_STATE = 64
ROPE_BASE = 10000.0
CONV_K = 4
CONV_PAD = 8
LANE = 128

TILE_MIX = 128
MIX_ROWS = 4
TILE_POST = 512
FF_BLOCK = 1024
VMEM_LIMIT_MIXER = 60 * 1024 * 1024
VMEM_LIMIT_POST = 56 * 1024 * 1024


def _rmsnorm_rows(x, w):
    y = x * lax.rsqrt(jnp.mean(x * x, axis=-1, keepdims=True) + NORM_EPS)
    return y * w


def _sigmoid(x):
    return 1.0 / (1.0 + jnp.exp(-x))


def _silu(x):
    hx = 0.5 * x
    return hx + hx * jnp.tanh(hx)


def _softplus(x):
    return jnp.maximum(x, 0.0) + jnp.log1p(jnp.exp(-jnp.abs(x)))


def _dot(a, b):
    return jnp.dot(a.astype(BF16), b.astype(BF16), preferred_element_type=F32)


def _dot_nt(a, b):
    return lax.dot_general(a.astype(BF16), b.astype(BF16), (((1,), (1,)), ((), ())),
                           preferred_element_type=F32)


def _dot_tn(a, b):
    return lax.dot_general(a.astype(BF16), b.astype(BF16), (((0,), (0,)), ((), ())),
                           preferred_element_type=F32)


def _chunk_cumsum(x, tri_ref):
    x1 = x.astype(BF16)
    x2 = (x - x1.astype(F32)).astype(BF16)
    tri = tri_ref[...]
    outs = []
    for s in range(x.shape[0] // 128):
        sl = slice(s * 128, (s + 1) * 128)
        outs.append(jnp.dot(tri, x1[sl], preferred_element_type=F32)
                    + jnp.dot(tri, x2[sl], preferred_element_type=F32))
    return jnp.concatenate(outs, axis=0)


def _iota(shape, dim):
    return lax.broadcasted_iota(jnp.int32, shape, dim)


def _block_diag_rows(y):
    yb = y.astype(BF16)
    half = yb.shape[1] // 2
    zero = jnp.zeros((yb.shape[0], half), BF16)
    low = _iota(zero.shape, 1) < HEAD_W
    blocks = []
    for e in range(4):
        tile = yb[:, (e // 2) * half:(e // 2 + 1) * half]
        kept = jnp.where(low if e % 2 == 0 else jnp.logical_not(low), tile, zero)
        blocks.append(jnp.concatenate([kept, zero] if e < 2 else [zero, kept], axis=1))
    return jnp.concatenate(blocks, axis=0)


def _expand_heads(cols, first_lane, rows):
    low = _iota((rows, 128), 1) < HEAD_W
    parts = []
    for p in range(N_HEADS // 2):
        a = jnp.broadcast_to(cols[:, first_lane + 2 * p:first_lane + 2 * p + 1], (rows, 128))
        b = jnp.broadcast_to(cols[:, first_lane + 2 * p + 1:first_lane + 2 * p + 2], (rows, 128))
        parts.append(jnp.where(low, a, b))
    return jnp.concatenate(parts, axis=1)


def _causal_conv(pre, carry_ref, w_ref, rows):
    cat = jnp.concatenate([carry_ref[...], pre], axis=0)
    acc = pre * w_ref[CONV_K - 1:CONV_K, :]
    for s in range(1, CONV_K):
        shifted = pltpu.roll(cat, s, 0)[CONV_PAD:CONV_PAD + rows, :]
        acc = acc + shifted * w_ref[CONV_K - 1 - s:CONV_K - s, :]
    carry_ref[...] = pre[rows - CONV_PAD:rows, :]
    return acc


def _head_outer(a, b):
    low = _iota((HEAD_W, 128), 1) < HEAD_W
    outs = []
    for p in range(2):
        m = _dot_tn(a[:, p * 128:(p + 1) * 128], b[:, p * 128:(p + 1) * 128])
        outs.append(jnp.where(low, m[0:HEAD_W], m[HEAD_W:2 * HEAD_W]))
    return jnp.concatenate(outs, axis=1)


def _head_sum(sq, ones_bd_ref):
    ones = ones_bd_ref[...]
    return jnp.concatenate(
        [jnp.dot(sq[:, i:i + QUAD_W].astype(BF16), ones, preferred_element_type=F32)
         for i in range(0, sq.shape[1], QUAD_W)], axis=1)


def _inv_unit_lower_quads(mats):
    shape = mats[0].shape
    r = _iota(shape, 0)
    c = _iota(shape, 1) % HEAD_W
    eye = jnp.where(r == c, 1.0, 0.0).astype(F32)
    in16 = (r // 16) == (c // 16)
    in32 = (r // 32) == (c // 32)
    off16 = jnp.logical_and(in32, jnp.logical_not(in16))

    def mm(xs, ys):
        return [jnp.dot(x.astype(BF16), _block_diag_rows(y), preferred_element_type=F32)
                for x, y in zip(xs, ys)]

    def mm2(xs, ps):
        outs = mm([jnp.concatenate([x, p], axis=0) for x, p in zip(xs, ps)], ps)
        n = shape[0]
        return [o[0:n] for o in outs], [o[n:2 * n] for o in outs]

    ns = [jnp.where(in16, -a, 0.0) for a in mats]
    xs = [eye + n for n in ns]
    ps = mm(ns, ns)
    yield
    for _ in range(2):
        xp, ps = mm2(xs, ps)
        xs = [x + d for x, d in zip(xs, xp)]
        yield
    xs = [x + d for x, d in zip(xs, mm(xs, ps))]
    yield
    for keep in (off16, jnp.logical_not(in32)):
        ls = [jnp.where(keep, a, 0.0) for a in mats]
        ts = mm(xs, ls)
        yield
        xs = [x - d for x, d in zip(xs, mm(ts, xs))]
        yield
    return xs


def _gdn_pieces(h, wqkv_ref, wz_ref, wsm_ref, convw_ref, alog_ref, dtb_ref, gnw_ref,
                ones_bd_ref, tri_ref, o_ref, cbuf_ref, state_ref, nb):
    T = TILE_MIX
    L = GDN_CHUNK
    n_chunks = T // L
    R = nb * T
    W = BRANCH_W

    def project(gi):
        return jnp.dot(h, wqkv_ref[:, gi * W:(gi + 1) * W], preferred_element_type=F32)

    def conv_silu(pre, gi):
        conv = jnp.concatenate(
            [_causal_conv(pre[b * T:(b + 1) * T], cbuf_ref.at[b, :, gi * W:(gi + 1) * W],
                          convw_ref.at[:, gi * W:(gi + 1) * W], T) for b in range(nb)], axis=0)
        return _silu(conv)

    pre_q = project(0)
    yield
    pre_k = project(1)
    yield
    q = conv_silu(pre_q, 0)
    yield
    pre_v = project(2)
    yield
    k = conv_silu(pre_k, 1)
    yield
    z = jnp.dot(h, wz_ref[...], preferred_element_type=F32)
    yield
    v = conv_silu(pre_v, 2)
    yield
    q = q * lax.rsqrt(_head_sum(q * q, ones_bd_ref) + 1e-6) * (HEAD_W ** -0.5)
    yield
    k = k * lax.rsqrt(_head_sum(k * k, ones_bd_ref) + 1e-6)
    yield

    small = jnp.dot(h, wsm_ref[...], preferred_element_type=F32)
    beta = _sigmoid(small)
    glog = -jnp.exp(alog_ref[...]) * _softplus(small + dtb_ref[...])
    gcum = _chunk_cumsum(glog, tri_ref)
    beta_x = _expand_heads(beta, 0, R)
    g_x = _expand_heads(gcum, N_HEADS, R)
    yield
    kb = k * beta_x
    vb = v * beta_x
    eg = jnp.exp(g_x)
    kbg = kb * eg
    qd = q * eg
    yield

    shape = (L, QUAD_W)
    r = _iota(shape, 0)
    c = _iota(shape, 1) % HEAD_W
    diag = r == c
    lower = r >= c
    strict = r > c

    def piece(t, b, ci, u):
        r0 = b * T + ci * L
        return t[r0:r0 + L, u * QUAD_W:(u + 1) * QUAD_W]

    items = [(b, ci, u) for ci in range(n_chunks) for b in range(nb) for u in range(2)]
    a_mat, attn = {}, {}
    for ci in range(n_chunks):
        its = [it for it in items if it[1] == ci]
        scs = [_dot_nt(jnp.concatenate([piece(kb, *it), piece(q, *it)], axis=0),
                       _block_diag_rows(piece(k, *it))) for it in its]
        for it, sc in zip(its, scs):
            gq = piece(g_x, *it)
            g_row = jnp.sum(jnp.where(diag, gq, 0.0), axis=0, keepdims=True)
            dec = jnp.exp(jnp.where(lower, gq - g_row, -jnp.inf))
            a_mat[it] = jnp.where(strict, sc[0:L] * dec, 0.0)
            attn[it] = (sc[L:2 * L] * dec).astype(BF16)
        yield
    tinv = yield from _inv_unit_lower_quads([a_mat[it] for it in items])
    uw = {}
    for it, ti in zip(items, tinv):
        rhs = jnp.concatenate([_block_diag_rows(piece(vb, *it)),
                               _block_diag_rows(piece(kbg, *it))], axis=1)
        uw[it] = jnp.dot(ti.astype(BF16), rhs, preferred_element_type=F32)
    yield

    chains = [(b, u) for b in range(nb) for u in range(2)]
    states = {ch: state_ref[ch[0], ch[1]] for ch in chains}
    outs = {}
    for ci in range(n_chunks):
        wq = {(b, u): jnp.dot(
            jnp.concatenate([uw[(b, ci, u)][:, QUAD_W:], piece(qd, b, ci, u)], axis=0).astype(BF16),
            _block_diag_rows(states[(b, u)]), preferred_element_type=F32) for b, u in chains}
        yield
        v_new = {(b, u): uw[(b, ci, u)][:, 0:QUAD_W] - wq[(b, u)][0:L] for b, u in chains}
        for b, u in chains:
            gq = piece(g_x, b, ci, u)
            g_last = gq[L - 1:L, :]
            upd = _head_outer(piece(k, b, ci, u) * jnp.exp(g_last - gq), v_new[(b, u)])
            states[(b, u)] = states[(b, u)] * jnp.exp(g_last) + upd
        yield
        for b, u in chains:
            outs[(b, ci, u)] = wq[(b, u)][L:2 * L] + jnp.dot(
                attn[(b, ci, u)], _block_diag_rows(v_new[(b, u)]), preferred_element_type=F32)
        yield
    for b, u in chains:
        state_ref[b, u] = states[(b, u)]

    o_all = jnp.concatenate(
        [jnp.concatenate([outs[(b, ci, 0)], outs[(b, ci, 1)]], axis=1)
         for b in range(nb) for ci in range(n_chunks)], axis=0)
    y = o_all * lax.rsqrt(_head_sum(o_all * o_all, ones_bd_ref) * (1.0 / HEAD_W) + NORM_EPS)
    y = y * gnw_ref[...] * _silu(z)
    o_ref[...] = y.reshape(nb, T, BRANCH_W).astype(o_ref.dtype)
    yield


def _ssd_pieces(h, wz_ref, wxbc_ref, wsm_ref, convw_ref, convb_ref, alog_ref, dtb_ref,
                dskip_ref, gnw_ref, tri_ref, o_ref, cbuf_ref, state_ref, nb):
    T = TILE_MIX
    L = SSM_CHUNK
    assert T == L
    R = nb * T
    C = BRANCH_W + 2 * SSM_GROUPS * SSM_STATE

    pre = jnp.dot(h, wxbc_ref[...], preferred_element_type=F32)
    yield
    z = jnp.dot(h, wz_ref[...], preferred_element_type=F32)
    yield
    half = C // 2
    parts = []
    for ci in range(2):
        sl = slice(ci * half, (ci + 1) * half)
        conv = jnp.concatenate(
            [_causal_conv(pre[b * T:(b + 1) * T, sl], cbuf_ref.at[b, :, sl], convw_ref.at[:, sl], T)
             for b in range(nb)], axis=0)
        parts.append(_silu(conv + convb_ref[:, sl]))
        yield
    xbc = jnp.concatenate(parts, axis=1)
    xs = xbc[:, 0:BRANCH_W]
    b_pair = xbc[:, BRANCH_W:BRANCH_W + 128]
    c_pair = xbc[:, BRANCH_W + 128:BRANCH_W + 256]
    small = jnp.dot(h, wsm_ref[...], preferred_element_type=F32)
    dt = _softplus(small + dtb_ref[...])
    a = -jnp.exp(alog_ref[...]) * dt
    acs = _chunk_cumsum(a, tri_ref)
    dt_x = _expand_heads(dt, 0, R)
    acs_x = _expand_heads(acs, 0, R)
    xdt = xs * dt_x
    eacs_x = jnp.exp(acs_x)
    yield

    r = _iota((L, L), 0)
    c = _iota((L, L), 1)
    diag = r == c
    lower = r >= c
    lane_low = _iota((L, 128), 1) < SSM_STATE

    items = [(b, g) for b in range(nb) for g in range(SSM_GROUPS)]

    def rows(t, b):
        return t[b * T:(b + 1) * T]

    def keep(t, g):
        return jnp.where(lane_low if g == 0 else jnp.logical_not(lane_low), t, 0.0)

    c_g = {(b, g): keep(rows(c_pair, b), g) for b, g in items}
    cb = {(b, g): _dot_nt(c_g[(b, g)], rows(b_pair, b)) for b, g in items}
    y_off = {(b, g): _dot(c_g[(b, g)], state_ref[b, g]) for b, g in items}
    yield
    ys = {}
    for b in range(nb):
        for g in range(SSM_GROUPS):
            parts = []
            for e in range(4):
                hd = 4 * g + e
                col = jnp.broadcast_to(rows(acs, b)[:, hd:hd + 1], (L, L))
                row = jnp.sum(jnp.where(diag, col, 0.0), axis=0, keepdims=True)
                parts.append(cb[(b, g)] * jnp.exp(jnp.where(lower, col - row, -jnp.inf)))
            m_quad = jnp.concatenate(parts, axis=1).astype(BF16)
            sl = slice(g * QUAD_W, (g + 1) * QUAD_W)
            xq = rows(xdt, b)[:, sl]
            acs_q = rows(acs_x, b)[:, sl]
            last_q = acs_q[L - 1:L, :]
            y_diag = jnp.dot(m_quad, _block_diag_rows(xq), preferred_element_type=F32)
            states = _dot_tn(keep(rows(b_pair, b), g), xq * jnp.exp(last_q - acs_q))
            state_ref[b, g] = state_ref[b, g] * jnp.exp(last_q) + states
            ys[(b, g)] = y_diag + y_off[(b, g)] * rows(eacs_x, b)[:, sl]
        yield
    y = jnp.concatenate([jnp.concatenate([ys[(b, 0)], ys[(b, 1)]], axis=1) for b in range(nb)], axis=0)
    y = (y + dskip_ref[...] * xs) * _silu(z)
    outs = []
    for g in range(SSM_GROUPS):
        yg = y[:, g * QUAD_W:(g + 1) * QUAD_W]
        outs.append(yg * lax.rsqrt(jnp.mean(yg * yg, axis=-1, keepdims=True) + NORM_EPS))
    y = jnp.concatenate(outs, axis=1) * gnw_ref[...]
    o_ref[...] = y.reshape(nb, T, BRANCH_W).astype(o_ref.dtype)
    yield


def _rotate_pairs(t, cos, sin_signed):
    n = t.shape[1]
    lo = (_iota(t.shape, 1) % HEAD_W) < (HEAD_W // 2)
    partner = jnp.where(lo, pltpu.roll(t, n - HEAD_W // 2, 1), pltpu.roll(t, HEAD_W // 2, 1))
    return t * cos + partner * sin_signed


def _ret_pieces(h, w_ref, cos_ref, sin_ref, dmat_ref, qdec_ref, kdec_ref, cgam_ref,
                ones_bd_ref, o_ref, state_ref, nb):
    T = TILE_MIX
    L = RET_CHUNK
    assert T == L
    W = BRANCH_W

    proj = []
    for gi in range(4):
        proj.append(jnp.dot(h, w_ref[:, gi * W:(gi + 1) * W], preferred_element_type=F32))
        yield
    cos = jnp.concatenate([jnp.concatenate([cos_ref[...]] * (W // LANE), axis=1)] * nb, axis=0)
    sin = jnp.concatenate([jnp.concatenate([sin_ref[...]] * (W // LANE), axis=1)] * nb, axis=0)
    q = _rotate_pairs(proj[0], cos, sin)
    yield
    k = _rotate_pairs(proj[1], cos, sin) * (HEAD_W ** -0.5)
    yield
    v = proj[2]
    gate = proj[3]

    items = [(b, u) for b in range(nb) for u in range(2)]

    def piece(t, b, u):
        return t[b * T:(b + 1) * T, u * QUAD_W:(u + 1) * QUAD_W]

    ys = {}
    for b0 in range(0, nb, 2):
        its = [it for it in items if b0 <= it[0] < b0 + 2]
        sc = {it: _dot_nt(piece(q, *it), _block_diag_rows(piece(k, *it))) for it in its}
        y_off = {}
        for b, u in its:
            sl = slice(u * QUAD_W, (u + 1) * QUAD_W)
            y_off[(b, u)] = jnp.dot((piece(q, b, u) * qdec_ref[:, sl]).astype(BF16),
                                    _block_diag_rows(state_ref[b, u]), preferred_element_type=F32)
        yield
        for b, u in its:
            sl = slice(u * QUAD_W, (u + 1) * QUAD_W)
            a = (sc[(b, u)] * dmat_ref[u]).astype(BF16)
            ys[(b, u)] = y_off[(b, u)] + jnp.dot(a, _block_diag_rows(piece(v, b, u)),
                                                 preferred_element_type=F32)
            kv = _head_outer(piece(k, b, u) * kdec_ref[:, sl], piece(v, b, u))
            state_ref[b, u] = state_ref[b, u] * cgam_ref[:, sl] + kv
        yield
    y = jnp.concatenate([jnp.concatenate([ys[(b, 0)], ys[(b, 1)]], axis=1) for b in range(nb)], axis=0)
    y = y * lax.rsqrt(_head_sum(y * y, ones_bd_ref) * (1.0 / HEAD_W) + NORM_EPS) * _silu(gate)
    o_ref[...] = y.reshape(nb, T, BRANCH_W).astype(o_ref.dtype)
    yield


_MIX_COLS = tuple(np.cumsum([0, 3 * BRANCH_W, BRANCH_W, SMALL_W, BRANCH_W,
                             BRANCH_W + 2 * SSM_GROUPS * SSM_STATE, SMALL_W, 4 * BRANCH_W]).tolist())

_PIECE_ORDER = ("AAAAAAA" "CACACACA" "BABA" "AGBAGCAGBAGCAGBAGBAGBAGBBAGBB" "AGCAGCBAGCACBACACBCC" "ABC")


def _gate_pieces(h, wg_ref, og_ref, nb):
    for lo in range(0, wg_ref.shape[1], QUAD_W):
        g = _sigmoid(jnp.dot(h, wg_ref[:, lo:lo + QUAD_W], preferred_element_type=F32))
        og_ref[:, :, lo:lo + QUAD_W] = g.reshape(nb, TILE_MIX, QUAD_W).astype(og_ref.dtype)
        yield


def _mixers_kernel(x_ref, nw_ref, w_ref, wg_ref,
                   a_convw, a_alog, a_dtb, a_gnw, a_tri,
                   b_convw, b_convb, b_alog, b_dtb, b_dskip, b_gnw, b_tri,
                   c_cos, c_sin, c_dmat, c_qdec, c_kdec, c_cgam, ones_bd_ref,
                   oa_ref, ob_ref, oc_ref, og_ref,
                   a_cbuf, a_state, b_cbuf, b_state, c_state, *, nb):
    R = nb * TILE_MIX

    @pl.when(pl.program_id(1) == 0)
    def _():
        for ref in (a_state, b_state, c_state, a_cbuf, b_cbuf):
            ref[...] = jnp.zeros_like(ref)

    h = _rmsnorm_rows(x_ref[...].reshape(R, D_MODEL), nw_ref[...]).astype(BF16)
    a_wqkv, a_wz, a_wsm, b_wz, b_wxbc, b_wsm, c_w = (
        w_ref.at[:, lo:hi] for lo, hi in zip(_MIX_COLS[:-1], _MIX_COLS[1:]))
    bounds = [(0, nb // 2), (nb // 2, nb)] if nb % 2 == 0 else [(0, nb)]
    gens = {"G": _gate_pieces(h, wg_ref, og_ref, nb)}
    for half, (lo, hi) in enumerate(bounds):
        hh = h[lo * TILE_MIX:hi * TILE_MIX]
        n = hi - lo
        gens[(half, "A")] = _gdn_pieces(hh, a_wqkv, a_wz, a_wsm, a_convw, a_alog, a_dtb, a_gnw,
                                        ones_bd_ref, a_tri, oa_ref.at[lo:hi], a_cbuf.at[lo:hi],
                                        a_state.at[lo:hi], n)
        gens[(half, "B")] = _ssd_pieces(hh, b_wz, b_wxbc, b_wsm, b_convw, b_convb, b_alog, b_dtb,
                                        b_dskip, b_gnw, b_tri, ob_ref.at[lo:hi], b_cbuf.at[lo:hi],
                                        b_state.at[lo:hi], n)
        gens[(half, "C")] = _ret_pieces(hh, c_w, c_cos, c_sin, c_dmat, c_qdec, c_kdec, c_cgam,
                                        ones_bd_ref, oc_ref.at[lo:hi], c_state.at[lo:hi], n)
    front = "AAAAAAA" "CACACACA" "BB" "CBCBB"
    back = "ABACABACABACAB" + "A" * 12
    order = [(0, t) for t in front]
    if len(bounds) == 2:
        f1, b0 = [(1, t) for t in front], [(0, t) for t in back]
        for i in range(max(len(f1), len(b0))):
            order += f1[i:i + 1] + b0[i:i + 1]
        for i, t in enumerate(back):
            order += [(1, t)] + (["G"] if i % 2 == 0 else [])
    else:
        order += [(0, t) for t in back]
    for tag in order:
        if tag in gens and next(gens[tag], "done") == "done":
            del gens[tag]
    for gen in gens.values():
        for _ in gen:
            pass


def _post_kernel(x_ref, ya_ref, yb_ref, yc_ref, g_ref, p_ref, wbr_ref, wout_ref,
                 nmlp_ref, wup_ref, wdown_ref, nple_ref, wpg_ref, wpp_ref, nfin_ref, o_ref,
                 *, final_norm):
    x = x_ref[...]
    mixed = None
    for n, y_ref in enumerate((ya_ref, yb_ref, yc_ref)):
        branch = jnp.dot(y_ref[...], wbr_ref[n], preferred_element_type=F32)
        term = g_ref[:, n * D_MODEL:(n + 1) * D_MODEL].astype(F32) * branch
        mixed = term if mixed is None else mixed + term
    x = x + jnp.dot(mixed.astype(BF16), wout_ref[...], preferred_element_type=F32)

    h = _rmsnorm_rows(x, nmlp_ref[...]).astype(BF16)
    acc = None
    for f in range(D_FF // FF_BLOCK):
        u = jnp.maximum(jnp.dot(h, wup_ref[:, f * FF_BLOCK:(f + 1) * FF_BLOCK],
                                preferred_element_type=F32), 0.0)
        d = jnp.dot((u * u).astype(BF16), wdown_ref[f * FF_BLOCK:(f + 1) * FF_BLOCK, :],
                    preferred_element_type=F32)
        acc = d if acc is None else acc + d
    x = x + acc

    h = _rmsnorm_rows(x, nple_ref[...]).astype(BF16)
    ple = jnp.dot(p_ref[...].astype(BF16), wpp_ref[...], preferred_element_type=F32)
    x = x + _sigmoid(jnp.dot(h, wpg_ref[...], preferred_element_type=F32)) * ple
    if final_norm:
        x = _rmsnorm_rows(x, nfin_ref[...])
    o_ref[...] = x


_SRC = {"a_small": 4 * BRANCH_W, "b": 4 * BRANCH_W + 2 * N_HEADS}
_SRC["b_small"] = _SRC["b"] + 2 * BRANCH_W + 2 * SSM_GROUPS * SSM_STATE
_SRC["c"] = _SRC["b_small"] + N_HEADS
MIX_SRC_W = -(-(_SRC["c"] + 4 * BRANCH_W) // LANE) * LANE
PREP_STEPS = 8


def _prep_kernel(win_ref, wbr_ref, wout_ref, wup_ref, wdown_ref, wpg_ref, wpp_ref,
                 mix_ref, obr_ref, oout_ref, oup_ref, odown_ref, opg_ref, opp_ref):
    for src_ref, dst_ref in ((wbr_ref, obr_ref), (wout_ref, oout_ref), (wup_ref, oup_ref),
                             (wdown_ref, odown_ref), (wpg_ref, opg_ref), (wpp_ref, opp_ref)):
        dst_ref[...] = src_ref[...].astype(dst_ref.dtype)

    w = win_ref[...].astype(F32)
    lane = _iota((w.shape[0], 128), 1)

    def put(dst_lo, width, src_lo, keep=None):
        lo = src_lo // 128 * 128
        hi = -(-(src_lo + width) // 128) * 128
        t = w[:, lo:hi]
        if src_lo != lo:
            t = pltpu.roll(t, (hi - lo) - (src_lo - lo), 1)
        t = t[:, 0:width]
        if keep is not None:
            t = jnp.where(lane < keep, t, 0.0)
        mix_ref[:, dst_lo:dst_lo + width] = t.astype(mix_ref.dtype)

    c = _MIX_COLS
    put(c[0], c[2] - c[0], 0)
    put(c[2], SMALL_W, _SRC["a_small"], keep=2 * N_HEADS)
    put(c[3], c[5] - c[3], _SRC["b"])
    put(c[5], SMALL_W, _SRC["b_small"], keep=N_HEADS)
    put(c[6], c[7] - c[6], _SRC["c"])


def _prepare_weights(w_in, w_branch, w_out, w_up, w_down, w_ple_gate, w_ple_proj):
    depth = w_in.shape[0]
    srcs = (w_in, w_branch.reshape(depth, -1, D_MODEL), w_out, w_up, w_down, w_ple_gate, w_ple_proj)
    widths = (MIX_SRC_W,) + tuple(a.shape[2] for a in srcs[1:])
    in_specs = [pl.BlockSpec((None, a.shape[1] // PREP_STEPS, wd), lambda l, r: (l, r, 0))
                for a, wd in zip(srcs, widths)]
    out_widths = (_MIX_COLS[-1],) + widths[1:]
    out_shape = [jax.ShapeDtypeStruct((depth, a.shape[1], wd), BF16) for a, wd in zip(srcs, out_widths)]
    out_specs = [pl.BlockSpec((None, a.shape[1] // PREP_STEPS, wd), lambda l, r: (l, r, 0))
                 for a, wd in zip(srcs, out_widths)]
    return pl.pallas_call(
        _prep_kernel, out_shape=out_shape, grid=(depth, PREP_STEPS),
        in_specs=in_specs, out_specs=out_specs,
        compiler_params=pltpu.CompilerParams(
            dimension_semantics=("arbitrary", "arbitrary"), vmem_limit_bytes=VMEM_LIMIT_POST),
        name="prepare_weights",
    )(*srcs)


def _const_spec(shape):
    zeros = (0,) * len(shape)
    return pl.BlockSpec(shape, lambda *_: zeros, pipeline_mode=pl.Buffered(1))


def _layer_spec(stacked, layer):
    zeros = (0,) * (stacked.ndim - 1)
    return pl.BlockSpec((None,) + stacked.shape[1:], lambda *_: (layer,) + zeros,
                        pipeline_mode=pl.Buffered(1))


def _ones_block_diag():
    i = np.arange(QUAD_W) // HEAD_W
    return jnp.asarray((i[:, None] == i[None, :]).astype(np.float32), dtype=BF16)


def _chunk_tri(rows, chunk):
    i = np.arange(rows)
    m = (i[:, None] >= i[None, :]) & ((i[:, None] // chunk) == (i[None, :] // chunk))
    return jnp.asarray(m.astype(np.float32), dtype=BF16)


def _lane_vec(v, first_lane):
    out = jnp.zeros((1, SMALL_W), F32)
    return out.at[0, first_lane:first_lane + v.shape[0]].set(v.astype(F32))


def _retention_tables(S):
    half = HEAD_W // 2
    inv_freq = ROPE_BASE ** (-jnp.linspace(0.0, 1.0, half, dtype=F32))
    ang = jnp.arange(S, dtype=F32)[:, None] * inv_freq[None, :]
    cos = jnp.tile(jnp.cos(ang), (1, LANE // half))
    sin = jnp.sin(ang)
    sin_signed = jnp.tile(jnp.concatenate([-sin, sin], axis=1), (1, LANE // HEAD_W))
    L = RET_CHUNK
    log_gamma = jnp.log1p(-jnp.exp2(-5.0 - jnp.arange(N_HEADS, dtype=F32)))
    idx = jnp.arange(L, dtype=F32)
    tril = jnp.tril(jnp.ones((L, L), bool))
    dmat = jnp.exp(jnp.where(tril, (idx[:, None] - idx[None, :]) * log_gamma[:, None, None], -jnp.inf))
    dmat = dmat.reshape(2, 4, L, L).transpose(0, 2, 1, 3).reshape(2, L, 4 * L)
    kdec = jnp.repeat(jnp.exp((L - 1 - idx)[:, None] * log_gamma[None, :]), HEAD_W, axis=1)
    qdec = jnp.repeat(jnp.exp((idx + 1)[:, None] * log_gamma[None, :]), HEAD_W, axis=1)
    cgam = jnp.repeat(jnp.exp(L * log_gamma)[None, :], HEAD_W, axis=1)
    return cos, sin_signed, dmat, qdec, kdec, cgam


def kernel(x, p, norm_mix, w_in, gdn_conv_w, gdn_A_log, gdn_dt_bias, gdn_norm_w, ssm_conv_w,
           ssm_conv_b, ssm_A_log, ssm_dt_bias, ssm_D, ssm_norm_w, w_branch, w_out, norm_mlp,
           w_up, w_down, norm_ple, w_ple_gate, w_ple_proj, norm_final):
    B, S, D = x.shape
    depth = w_in.shape[0]
    assert D == D_MODEL and S % TILE_MIX == 0 and B % min(B, MIX_ROWS) == 0 and (B * S) % TILE_POST == 0
    ones_bd = _ones_block_diag()
    tri_gdn = _chunk_tri(128, GDN_CHUNK)
    tri_ssd = _chunk_tri(128, SSM_CHUNK)
    cos, sin_signed, dmat, qdec, kdec, cgam = _retention_tables(S)
    row = lambda v: v.reshape(1, -1).astype(F32)

    sizes = (3 * BRANCH_W, BRANCH_W, N_HEADS, N_HEADS, BRANCH_W, BRANCH_W + 2 * SSM_GROUPS * SSM_STATE,
             N_HEADS, BRANCH_W, BRANCH_W, BRANCH_W, BRANCH_W, 3 * D_MODEL)
    offs = np.concatenate([[0], np.cumsum(sizes)])

    w_in16 = w_in.astype(BF16)
    w_gate = w_in16[:, :, offs[11]:offs[12]]
    w_mix, wb_branch, wb_out, wb_up, wb_down, wb_pg, wb_pp = _prepare_weights(
        w_in16, w_branch, w_out, w_up, w_down, w_ple_gate, w_ple_proj)
    wb_branch = wb_branch.reshape(depth, w_branch.shape[1], w_branch.shape[2], D)

    for i in range(depth):
        nmix = row(norm_mix[i])
        consts_a = (gdn_conv_w[i].astype(F32), _lane_vec(gdn_A_log[i], N_HEADS),
                    _lane_vec(gdn_dt_bias[i], N_HEADS), row(jnp.tile(gdn_norm_w[i], N_HEADS)), tri_gdn)
        consts_b = (ssm_conv_w[i].astype(F32), row(ssm_conv_b[i]), _lane_vec(ssm_A_log[i], 0),
                    _lane_vec(ssm_dt_bias[i], 0), row(jnp.repeat(ssm_D[i], HEAD_W)), row(ssm_norm_w[i]),
                    tri_ssd)
        consts_c = (dmat, qdec, kdec, cgam, ones_bd)
        nb = min(B, MIX_ROWS)
        pos_spec = pl.BlockSpec((TILE_MIX, LANE), lambda g, j: (j, 0))
        y_spec = pl.BlockSpec((nb, TILE_MIX, BRANCH_W), lambda g, j: (g, j, 0))
        y_shape = jax.ShapeDtypeStruct((B, S, BRANCH_W), BF16)
        xbc_w = BRANCH_W + 2 * SSM_GROUPS * SSM_STATE
        g_shape = jax.ShapeDtypeStruct((B, S, 3 * D), BF16)
        g_spec = pl.BlockSpec((nb, TILE_MIX, 3 * D), lambda g, j: (g, j, 0))
        y_a, y_b, y_c, gates = pl.pallas_call(
            functools.partial(_mixers_kernel, nb=nb),
            out_shape=(y_shape, y_shape, y_shape, g_shape),
            grid=(B // nb, S // TILE_MIX),
            in_specs=[pl.BlockSpec((nb, TILE_MIX, D), lambda g, j: (g, j, 0)), _const_spec(nmix.shape),
                      _layer_spec(w_mix, i), _layer_spec(w_gate, i)]
            + [_const_spec(c.shape) for c in consts_a + consts_b] + [pos_spec, pos_spec]
            + [_const_spec(c.shape) for c in consts_c],
            out_specs=(y_spec, y_spec, y_spec, g_spec),
            scratch_shapes=[pltpu.VMEM((nb, CONV_PAD, 3 * BRANCH_W), F32),
                            pltpu.VMEM((nb, 2, HEAD_W, QUAD_W), F32),
                            pltpu.VMEM((nb, CONV_PAD, xbc_w), F32),
                            pltpu.VMEM((nb, SSM_GROUPS, 128, QUAD_W), F32),
                            pltpu.VMEM((nb, 2, HEAD_W, QUAD_W), F32)],
            compiler_params=pltpu.CompilerParams(
                dimension_semantics=("arbitrary", "arbitrary"), vmem_limit_bytes=VMEM_LIMIT_MIXER),
            name="mixers",
        )(x, nmix, w_mix, w_gate, *consts_a, *consts_b, cos, sin_signed, *consts_c)
        n_tok = B * S
        tok = lambda a: a.reshape(n_tok, a.shape[-1])
        consts = (wb_branch, wb_out, row(norm_mlp[i]), wb_up, wb_down,
                  row(norm_ple[i]), wb_pg, wb_pp, row(norm_final))
        stacked = (0, 1, 3, 4, 6, 7)
        tile_spec = lambda width: pl.BlockSpec((TILE_POST, width), lambda t: (t, 0))
        p_spec = pl.BlockSpec((None, TILE_POST, PLE_DIM), lambda t, layer=i: (layer, t, 0))
        x = pl.pallas_call(
            functools.partial(_post_kernel, final_norm=(i == depth - 1)),
            out_shape=jax.ShapeDtypeStruct((n_tok, D), F32),
            grid=(n_tok // TILE_POST,),
            in_specs=[tile_spec(D), tile_spec(BRANCH_W), tile_spec(BRANCH_W), tile_spec(BRANCH_W),
                      tile_spec(3 * D), p_spec] + [_layer_spec(c, i) if n in stacked else _const_spec(c.shape)
                                 for n, c in enumerate(consts)],
            out_specs=tile_spec(D),
            compiler_params=pltpu.CompilerParams(
                dimension_semantics=("arbitrary",), vmem_limit_bytes=VMEM_LIMIT_POST),
            name="merge_mlp_ple",
        )(tok(x), tok(y_a), tok(y_b), tok(y_c), tok(gates), p.reshape(depth, n_tok, PLE_DIM),
          *consts).reshape(B, S, D)
    return x
```
